```python
import jax, jax.numpy as jnp
from jax import lax
import numpy as np

D_MODEL = 2048
BATCH = 8
SEQ = 2048
DEPTH = 2

H_A = 8
DH_A = 128
D_C = 256
H_I = 8
D_I = 64
INDEX_TOPK = 256
Q_BLOCK = 128
H_B = 4
DK_B = 128
DV_B = 256
RET_CHUNK = 128
RET_THETA = 10000.0
H_QK = 16
H_V = 32
DK_C = 128
DV_C = 128
CONV_K = 4
DN_CHUNK = 64

EPS = 1e-6
MIX_A = H_A * DH_A
MIX_B = H_B * DV_B
MIX_AB = MIX_A + MIX_B
MIX_C = H_V * DV_C
CONV_CH = 2 * H_QK * DK_C + MIX_C
AB_SPLITS = (MIX_A, D_C, H_I * D_I, D_I, H_I, H_B * DK_B, H_B * DK_B, MIX_B, MIX_AB)
AB_COLS = sum(AB_SPLITS)
DN_SPLITS = (CONV_CH, MIX_C, H_V, H_V)
DN_COLS = sum(DN_SPLITS)
N_EVEN = (DEPTH + 1) // 2
N_ODD = DEPTH // 2

kernel_name = "hybrid_dsa_retention_gdn"


def _split(a, sizes):
    return jnp.split(a, [int(v) for v in np.cumsum(sizes)[:-1]], axis=-1)


def rms_norm(x, g):
    x = x.astype(jnp.float32)
    return x * lax.rsqrt(jnp.mean(x * x, axis=-1, keepdims=True) + EPS) * g


def l2_norm(x):
    return x * lax.rsqrt(jnp.sum(x * x, axis=-1, keepdims=True) + EPS)


def dsa_attention(q, c, q_idx, k_idx, w_idx, kv_norm, w_uk, w_uv, q_norm, k_norm):
    B, S = c.shape[:2]
    topk = min(INDEX_TOPK, S // 4)
    c = rms_norm(c, kv_norm)
    k = rms_norm(c @ w_uk, k_norm)
    q = rms_norm(q, q_norm)
    w_idx = w_idx * (H_I ** -0.5 * D_I ** -0.5)
    nblk = S // Q_BLOCK

    def to_blocks(a):
        return a.reshape(B, nblk, Q_BLOCK, *a.shape[2:]).swapaxes(0, 1)

    gather = jax.vmap(lambda a, i: a[i])
    s_pos = jnp.arange(S)

    def block(args):
        qb, qib, wb, start = args
        t = start + jnp.arange(Q_BLOCK)
        causal = s_pos[None, :] <= t[:, None]
        iscore = jax.nn.relu(jnp.einsum('bqhd,bsd->bqhs', qib, k_idx))
        iscore = jnp.einsum('bqhs,bqh->bqs', iscore, wb)
        iscore = jnp.where(causal[None], iscore, -jnp.inf)
        _, idx = lax.top_k(iscore, topk)
        valid = idx <= t[None, :, None]
        k_sel = gather(k, idx)
        c_sel = gather(c, idx)
        logits = jnp.einsum('bqhd,bqkd->bqhk', qb, k_sel) * (DH_A ** -0.5)
        logits = jnp.where(valid[:, :, None, :], logits, -jnp.inf)
        p = jax.nn.softmax(logits, axis=-1)
        lat = jnp.einsum('bqhk,bqkc->bqhc', p, c_sel)
        return jnp.einsum('bqhc,hcd->bqhd', lat, w_uv)

    out = lax.map(block, (to_blocks(q), to_blocks(q_idx), to_blocks(w_idx),
                          jnp.arange(nblk) * Q_BLOCK))
    return out.swapaxes(0, 1).reshape(B, S, MIX_A)


def _rotary(x, cos, sin):
    x1, x2 = jnp.split(x, 2, axis=-1)
    return jnp.concatenate([x1 * cos - x2 * sin, x2 * cos + x1 * sin], axis=-1)


def retention(q, k, v, ret_norm):
    B, S = q.shape[:2]
    C = RET_CHUNK
    N = S // C
    pos = jnp.arange(S, dtype=jnp.float32)
    inv_freq = 1.0 / (RET_THETA ** jnp.linspace(0.0, 1.0, DK_B // 2, dtype=jnp.float32))
    ang = pos[:, None] * inv_freq[None, :]
    cos = jnp.cos(ang)[None, :, None, :]
    sin = jnp.sin(ang)[None, :, None, :]
    q = _rotary(q, cos, sin)
    k = _rotary(k, cos, sin) * (DK_B ** -0.5)
    log_gamma = jnp.log1p(-jnp.exp2(-5.0 - jnp.arange(H_B, dtype=jnp.float32)))

    def chunks(a):
        return a.reshape(B, N, C, H_B, a.shape[-1]).transpose(0, 3, 1, 2, 4)

    q, k, v = chunks(q), chunks(k), chunks(v.astype(jnp.float32))
    i = jnp.arange(C, dtype=jnp.float32)
    diff = i[:, None] - i[None, :]
    dmask = jnp.where(diff >= 0, jnp.exp(jnp.maximum(diff, 0.0)[None] * log_gamma[:, None, None]), 0.0)
    inner = jnp.einsum('bhncd,bhnjd->bhncj', q, k) * dmask[None, :, None]
    inner = jnp.einsum('bhncj,bhnje->bhnce', inner, v)
    lg = log_gamma[None, :, None, None, None]
    k_dec = k * jnp.exp((C - 1 - i)[:, None] * lg)
    kv = jnp.einsum('bhncd,bhnce->nbhde', k_dec, v)
    chunk_decay = jnp.exp(C * log_gamma)[None, :, None, None]

    def step(state, kv_n):
        return state * chunk_decay + kv_n, state

    _, prev = lax.scan(step, jnp.zeros((B, H_B, DK_B, DV_B), jnp.float32), kv)
    q_dec = q * jnp.exp((i + 1)[:, None] * lg)
    cross = jnp.einsum('bhncd,nbhde->bhnce', q_dec, prev)
    out = (inner + cross).transpose(0, 2, 3, 1, 4).reshape(B, S, H_B, DV_B)
    out = rms_norm(out, ret_norm.reshape(H_B, DV_B))
    return out.reshape(B, S, MIX_B)


def causal_conv(x, w):
    w = w.astype(x.dtype)
    return lax.conv_general_dilated(
        x, w[:, None, :], window_strides=(1,), padding=[(CONV_K - 1, 0)],
        dimension_numbers=('NWC', 'WIO', 'NWC'), feature_group_count=x.shape[-1])


def gated_delta_rule(q, k, v, beta, g):
    B, S = q.shape[:2]
    C = DN_CHUNK
    N = S // C

    def chunks(a):
        if a.ndim == 4:
            return a.reshape(B, N, C, H_V, a.shape[-1]).transpose(0, 3, 1, 2, 4)
        return a.reshape(B, N, C, H_V).transpose(0, 3, 1, 2)

    q = chunks(q) * (DK_C ** -0.5)
    k, v, beta = chunks(k), chunks(v), chunks(beta)
    g = jnp.cumsum(chunks(g), axis=-1)
    idx = jnp.arange(C)
    tri = idx[:, None] >= idx[None, :]
    strict = idx[:, None] > idx[None, :]
    decay = jnp.exp(jnp.where(tri, g[..., :, None] - g[..., None, :], -jnp.inf))
    k_beta = k * beta[..., None]
    v_beta = v * beta[..., None]
    lmat = jnp.where(strict, jnp.einsum('bhncd,bhnjd->bhncj', k_beta, k) * decay, 0.0)
    amat = lmat + jnp.eye(C, dtype=jnp.float32)
    u = lax.linalg.triangular_solve(amat, v_beta, left_side=True, lower=True, unit_diagonal=True)
    w = lax.linalg.triangular_solve(amat, k_beta * jnp.exp(g)[..., None],
                                    left_side=True, lower=True, unit_diagonal=True)
    attn = jnp.einsum('bhncd,bhnjd->bhncj', q, k) * decay
    g_last = g[..., -1]
    k_tail = k * jnp.exp(g_last[..., None] - g)[..., None]
    q_dec = q * jnp.exp(g)[..., None]
    xs = tuple(jnp.moveaxis(a, 2, 0) for a in (u, w, q_dec, k_tail, attn, jnp.exp(g_last)))

    def step(state, xs_n):
        u_n, w_n, qd_n, kt_n, at_n, gl_n = xs_n
        v_new = u_n - jnp.einsum('bhcd,bhde->bhce', w_n, state)
        o = jnp.einsum('bhcd,bhde->bhce', qd_n, state) + jnp.einsum('bhcj,bhje->bhce', at_n, v_new)
        state = state * gl_n[..., None, None] + jnp.einsum('bhcd,bhce->bhde', kt_n, v_new)
        return state, o

    _, o = lax.scan(step, jnp.zeros((B, H_V, DK_C, DV_C), jnp.float32), xs)
    return o.transpose(1, 0, 3, 2, 4).reshape(B, S, H_V, DV_C)


def setup_inputs(seed: int = 0) -> dict:
    key = jax.random.key(seed)
    ks = jax.random.split(key, 17)
    f32 = jnp.float32

    def normal(k, shape, scale):
        return scale * jax.random.normal(k, shape, f32)

    def gain(k, shape):
        return 1.0 + 0.1 * jax.random.normal(k, shape, f32)

    E, O = N_EVEN, N_ODD
    return {
        "x": normal(ks[0], (BATCH, SEQ, D_MODEL), 1.0),
        "ab_norm": gain(ks[1], (E, D_MODEL)),
        "ab_w_in": normal(ks[2], (E, D_MODEL, AB_COLS), D_MODEL ** -0.5),
        "ab_kv_norm": gain(ks[3], (E, D_C)),
        "ab_w_uk": normal(ks[4], (E, D_C, DH_A), D_C ** -0.5),
        "ab_w_uv": normal(ks[5], (E, H_A, D_C, DH_A), D_C ** -0.5),
        "ab_q_norm": gain(ks[6], (E, DH_A)),
        "ab_k_norm": gain(ks[7], (E, DH_A)),
        "ab_ret_norm": gain(ks[8], (E, MIX_B)),
        "ab_w_out": normal(ks[9], (E, MIX_AB, D_MODEL), MIX_AB ** -0.5),
        "dn_norm": gain(ks[10], (O, D_MODEL)),
        "dn_w_in": normal(ks[11], (O, D_MODEL, DN_COLS), D_MODEL ** -0.5),
        "dn_conv": normal(ks[12], (O, CONV_K, CONV_CH), CONV_K ** -0.5),
        "dn_a_log": jnp.log(jax.random.uniform(ks[13], (O, H_V), f32, 1.0, 16.0)),
        "dn_dt_bias": normal(ks[14], (O, H_V), 0.1),
        "dn_out_norm": gain(ks[15], (O, DV_C)),
        "dn_w_out": normal(ks[16], (O, MIX_C, D_MODEL), MIX_C ** -0.5),
    }


def reference(x, ab_norm, ab_w_in, ab_kv_norm, ab_w_uk, ab_w_uv, ab_q_norm, ab_k_norm,
              ab_ret_norm, ab_w_out, dn_norm, dn_w_in, dn_conv, dn_a_log, dn_dt_bias,
              dn_out_norm, dn_w_out):
    B, S, _ = x.shape
    h = x.astype(jnp.float32)
    for layer in range(DEPTH):
        j = layer // 2
        if layer % 2 == 0:
            proj = rms_norm(h, ab_norm[j]) @ ab_w_in[j]
            q_a, c, q_i, k_i, w_i, q_b, k_b, v_b, gate = _split(proj, AB_SPLITS)
            a_out = dsa_attention(q_a.reshape(B, S, H_A, DH_A), c,
                                  q_i.reshape(B, S, H_I, D_I), k_i, w_i,
                                  ab_kv_norm[j], ab_w_uk[j], ab_w_uv[j],
                                  ab_q_norm[j], ab_k_norm[j])
            b_out = retention(q_b.reshape(B, S, H_B, DK_B), k_b.reshape(B, S, H_B, DK_B),
                              v_b.reshape(B, S, H_B, DV_B), ab_ret_norm[j])
            y = jnp.concatenate([a_out, b_out], axis=-1) * jax.nn.silu(gate)
            h = h + y @ ab_w_out[j]
        else:
            proj = rms_norm(h, dn_norm[j]) @ dn_w_in[j]
            qkv, z, b_log, a_in = _split(proj, DN_SPLITS)
            qkv = jax.nn.silu(causal_conv(qkv, dn_conv[j]))
            q, k, v = _split(qkv, (H_QK * DK_C, H_QK * DK_C, MIX_C))
            q = jnp.repeat(l2_norm(q.reshape(B, S, H_QK, DK_C)), H_V // H_QK, axis=2)
            k = jnp.repeat(l2_norm(k.reshape(B, S, H_QK, DK_C)), H_V // H_QK, axis=2)
            v = v.reshape(B, S, H_V, DV_C)
            beta = jax.nn.sigmoid(b_log)
            g = -jnp.exp(dn_a_log[j].astype(jnp.float32)) * jax.nn.softplus(a_in + dn_dt_bias[j])
            o = gated_delta_rule(q, k, v, beta, g)
            o = rms_norm(o, dn_out_norm[j]) * jax.nn.silu(z.reshape(B, S, H_V, DV_C))
            h = h + o.reshape(B, S, MIX_C) @ dn_w_out[j]
    return h.astype(x.dtype)
```

```python
import functools

import numpy as np
import jax
import jax.numpy as jnp
from jax import lax
from jax.experimental import pallas as pl
from jax.experimental.pallas import tpu as pltpu

F32 = jnp.float32
BF16 = jnp.bfloat16
I32 = jnp.int32

EPS = 1e-6
H_A, DH_A, D_C, H_I, D_I = 8, 128, 256, 8, 64
INDEX_TOPK = 256
H_B, DK_B, DV_B, RET_CHUNK, RET_THETA = 4, 128, 256, 128, 10000.0
H_QK, H_V, DK_C, DV_C, CONV_K, DN_CHUNK = 16, 32, 128, 128, 4, 64

MIX_A = H_A * DH_A
MIX_B = H_B * DV_B
MIX_AB = MIX_A + MIX_B
MIX_C = H_V * DV_C

LANES = 128
INT_MIN = np.int32(-2**31)

AB_QA, AB_VB, AB_GATE, AB_QI, AB_QB, AB_KB, AB_C, AB_KW, AB_NP = 0, 1024, 2048, 4096, 4608, 5120, 5632, 5888, 6144
DN_Q, DN_K, DN_V, DN_Z, DN_BA, DN_NP = 0, 2048, 4096, 8192, 12288, 12800

VMEM_LIMIT = 56 * 1024 * 1024


def _sigmoid(x):
    return 1.0 / (1.0 + jnp.exp(-x))


def _silu(x):
    return x * _sigmoid(x)


def _norm_proj_kernel(x_ref, g_ref, w_ref, o_ref, xn_ref):
    @pl.when(pl.program_id(1) == 0)
    def _():
        x = x_ref[...]
        ms = jnp.mean(x * x, axis=-1, keepdims=True)
        xn_ref[...] = (x * lax.rsqrt(ms + EPS) * g_ref[...]).astype(BF16)

    o_ref[...] = jnp.dot(xn_ref[...], w_ref[...], preferred_element_type=F32)


def _norm_proj(x2d, gain, w, tm, tn):
    T, D = x2d.shape
    N = w.shape[1]
    return pl.pallas_call(
        _norm_proj_kernel,
        grid=(T // tm, N // tn),
        in_specs=[
            pl.BlockSpec((tm, D), lambda i, j: (i, 0)),
            pl.BlockSpec((1, D), lambda i, j: (0, 0)),
            pl.BlockSpec((D, tn), lambda i, j: (0, j)),
        ],
        out_specs=pl.BlockSpec((tm, tn), lambda i, j: (i, j)),
        out_shape=jax.ShapeDtypeStruct((T, N), F32),
        scratch_shapes=[pltpu.VMEM((tm, D), BF16)],
        compiler_params=pltpu.CompilerParams(
            dimension_semantics=("parallel", "arbitrary"), vmem_limit_bytes=VMEM_LIMIT),
        name="norm_proj",
    )(x2d, gain, w)


def _dsa_kernel(qa_ref, c_ref, qi_ref, kwq_ref, kwk_ref, kvn_ref, wuk_ref, kn_ref, qn_ref, wuv_ref,
                o_ref, cnT_s, k_s, ki_s, key_s, bias_s, lg_s, j_s, *, S, Q, topk):
    qb = pl.program_id(1)
    nkb = qb + 1
    RB = 64
    nrb = nkb * (Q // RB)
    NT = (((1,), (1,)), ((), ()))

    @pl.when(qb == 0)
    def _prep():
        def body(j, _):
            r0 = pl.multiple_of(j * Q, Q)
            c = c_ref[pl.ds(r0, Q), :]
            cn = c * lax.rsqrt(jnp.mean(c * c, axis=-1, keepdims=True) + EPS) * kvn_ref[...]
            kk = jnp.dot(cn.astype(BF16), wuk_ref[...], preferred_element_type=F32)
            kk = kk * lax.rsqrt(jnp.mean(kk * kk, axis=-1, keepdims=True) + EPS) * kn_ref[...]
            cnT_s[j] = cn.T.astype(BF16)
            k_s[pl.ds(r0, Q), :] = kk.astype(BF16)
            ki_s[pl.ds(r0, Q), :] = kwk_ref[pl.ds(r0, Q), 0:D_I].astype(BF16)
            return 0
        lax.fori_loop(0, S // Q, body, 0)

    tpos = qb * Q + lax.broadcasted_iota(I32, (1, Q), 1)
    w_t = kwq_ref[...].T[D_I:D_I + H_I, :] * (H_I ** -0.5 * D_I ** -0.5)

    def score_blk(j, _):
        r0 = pl.multiple_of(j * Q, Q)
        kib = ki_s[pl.ds(r0, Q), :]
        acc = jnp.zeros((Q, Q), F32)
        for h in range(H_I):
            qih = qi_ref[:, h * D_I:(h + 1) * D_I].astype(BF16)
            s = lax.dot_general(kib, qih, NT, preferred_element_type=F32)
            acc = acc + jnp.maximum(s, 0.0) * w_t[h:h + 1, :]
        acc = acc + 0.0
        bits = pltpu.bitcast(acc, I32)
        key = bits ^ ((bits >> 31) & np.int32(0x7FFFFFFF))
        kpos = r0 + lax.broadcasted_iota(I32, (Q, Q), 0)
        key_s[pl.ds(r0, Q), :] = jnp.where(kpos <= tpos, key, INT_MIN)
        return 0
    lax.fori_loop(0, nkb, score_blk, 0)

    def count(pred):
        def body(i, acc):
            r0 = pl.multiple_of(i * RB, RB)
            kpos = r0 + lax.broadcasted_iota(I32, (RB, Q), 0)
            return acc + jnp.where(pred(key_s[pl.ds(r0, RB), :], kpos), 1.0, 0.0)
        acc = lax.fori_loop(0, nrb, body, jnp.zeros((RB, Q), F32))
        return jnp.sum(acc, axis=0, keepdims=True)

    def bit_body(i, prefix_u):
        cand_u = prefix_u | jnp.left_shift(np.int32(1), 31 - i)
        cand_s = cand_u ^ INT_MIN
        cnt = count(lambda k, _: k >= cand_s)
        return jnp.where(cnt >= float(topk), cand_u, prefix_u)
    thr = lax.fori_loop(0, 32, bit_body, jnp.zeros((1, Q), I32)) ^ INT_MIN

    cnt_ge = count(lambda k, _: k >= thr)
    ambiguous = (cnt_ge != float(topk)) & (tpos >= topk - 1)
    j_s[...] = jnp.full((1, Q), S, I32)

    @pl.when(jnp.max(jnp.where(ambiguous, 1.0, 0.0)) > 0.0)
    def _ties():
        need = float(topk) - count(lambda k, _: k > thr)
        nbits = int(S).bit_length() - 1

        def jbit(i, prefix):
            cand = prefix | jnp.left_shift(np.int32(1), nbits - 1 - i)
            cnt = count(lambda k, kpos: (k == thr) & (kpos < cand))
            return jnp.where(cnt < need, cand, prefix)
        jstar = lax.fori_loop(0, nbits, jbit, jnp.zeros((1, Q), I32))
        j_s[...] = jnp.where(ambiguous, jstar, S)

    jsel = j_s[...]

    def bias_blk(i, _):
        r0 = pl.multiple_of(i * RB, RB)
        k = key_s[pl.ds(r0, RB), :]
        kpos = r0 + lax.broadcasted_iota(I32, (RB, Q), 0)
        sel = (kpos <= tpos) & ((k > thr) | ((k == thr) & (kpos <= jsel)))
        bias_s[pl.ds(r0, RB), :] = jnp.where(sel, 0.0, -jnp.inf)
        return 0
    lax.fori_loop(0, nrb, bias_blk, 0)

    for h in range(H_A):
        qh = qa_ref[:, h * DH_A:(h + 1) * DH_A]
        qh = qh * lax.rsqrt(jnp.mean(qh * qh, axis=-1, keepdims=True) + EPS) * qn_ref[...] * (DH_A ** -0.5)
        qh = qh.astype(BF16)

        def pass_a(j, m):
            r0 = pl.multiple_of(j * Q, Q)
            s = lax.dot_general(k_s[pl.ds(r0, Q), :], qh, NT, preferred_element_type=F32)
            s = s + bias_s[pl.ds(r0, Q), :]
            lg_s[pl.ds(r0, Q), :] = s
            return jnp.maximum(m, jnp.max(s, axis=0, keepdims=True))
        m = lax.fori_loop(0, nkb, pass_a, jnp.full((1, Q), -jnp.inf, F32))

        def pass_b(j, carry):
            l, lat = carry
            r0 = pl.multiple_of(j * Q, Q)
            p = jnp.exp(lg_s[pl.ds(r0, Q), :] - m)
            l = l + jnp.sum(p, axis=0, keepdims=True)
            lat = lat + jnp.dot(cnT_s[j], p.astype(BF16), preferred_element_type=F32)
            return l, lat
        l, lat = lax.fori_loop(0, nkb, pass_b, (jnp.zeros((1, Q), F32), jnp.zeros((D_C, Q), F32)))
        lat = (lat * (1.0 / l)).astype(BF16)
        out_h = lax.dot_general(lat, wuv_ref[h], (((0,), (0,)), ((), ())), preferred_element_type=F32)
        o_ref[:, h * DH_A:(h + 1) * DH_A] = out_h


def _dsa(proj, kv_norm, w_uk, k_norm, q_norm, w_uv, Q=256):
    B, S, _ = proj.shape
    topk = min(INDEX_TOPK, S // 4)
    assert S % Q == 0 and topk <= Q and (S & (S - 1)) == 0
    kern = functools.partial(_dsa_kernel, S=S, Q=Q, topk=topk)
    full = lambda *shape: pl.BlockSpec(shape, lambda b, q: (0,) * len(shape))
    return pl.pallas_call(
        kern,
        grid=(B, S // Q),
        in_specs=[
            pl.BlockSpec((None, Q, MIX_A), lambda b, q: (b, q, AB_QA // MIX_A)),
            pl.BlockSpec((None, S, D_C), lambda b, q: (b, 0, AB_C // D_C)),
            pl.BlockSpec((None, Q, H_I * D_I), lambda b, q: (b, q, AB_QI // (H_I * D_I))),
            pl.BlockSpec((None, Q, LANES), lambda b, q: (b, q, AB_KW // LANES)),
            pl.BlockSpec((None, S, LANES), lambda b, q: (b, 0, AB_KW // LANES)),
            full(1, D_C), full(D_C, DH_A), full(1, DH_A), full(1, DH_A), full(H_A, D_C, DH_A),
        ],
        out_specs=pl.BlockSpec((None, Q, MIX_A), lambda b, q: (b, q, 0)),
        out_shape=jax.ShapeDtypeStruct((B, S, MIX_A), F32),
        scratch_shapes=[
            pltpu.VMEM((S // Q, D_C, Q), BF16),
            pltpu.VMEM((S, DH_A), BF16),
            pltpu.VMEM((S, D_I), BF16),
            pltpu.VMEM((S, Q), I32),
            pltpu.VMEM((S, Q), F32),
            pltpu.VMEM((S, Q), F32),
            pltpu.VMEM((1, Q), I32),
        ],
        compiler_params=pltpu.CompilerParams(
            dimension_semantics=("parallel", "arbitrary"), vmem_limit_bytes=VMEM_LIMIT),
        name="dsa_attention",
    )(proj, proj, proj, proj, proj, kv_norm, w_uk, k_norm, q_norm, w_uv)


def _ret_kernel(lg_ref, q_ref, k_ref, v_ref, cos_ref, sin_ref, g_ref, o_ref, st_s, *, S, C):
    h = pl.program_id(1)
    lg = lg_ref[h]
    NT = (((1,), (1,)), ((), ()))
    ii = lax.broadcasted_iota(I32, (C, C), 0)
    jj = lax.broadcasted_iota(I32, (C, C), 1)
    diff = (ii - jj).astype(F32)
    dmask = jnp.where(diff >= 0, jnp.exp(jnp.maximum(diff, 0.0) * lg), 0.0)
    icol = lax.broadcasted_iota(I32, (C, 1), 0).astype(F32)
    k_dec = jnp.exp((C - 1 - icol) * lg)
    q_dec = jnp.exp((icol + 1.0) * lg)
    chunk_decay = jnp.exp(jnp.full((1, DV_B), float(C), F32) * lg)
    st_s[...] = jnp.zeros_like(st_s)

    def rot(x, cos, sin):
        return x * cos + pltpu.roll(x, DK_B // 2, axis=1) * sin

    def body(n, _):
        r0 = pl.multiple_of(n * C, C)
        cos = cos_ref[pl.ds(r0, C), :]
        sin = sin_ref[pl.ds(r0, C), :]
        q = rot(q_ref[pl.ds(r0, C), :], cos, sin)
        k = rot(k_ref[pl.ds(r0, C), :], cos, sin) * (DK_B ** -0.5)
        v = v_ref[pl.ds(r0, C), :]
        st = st_s[...]
        inner = lax.dot_general(q, k, NT, preferred_element_type=F32) * dmask
        out = jnp.dot(inner, v, preferred_element_type=F32)
        out = out + jnp.dot(q * q_dec, st, preferred_element_type=F32)
        kv = lax.dot_general(k * k_dec, v, (((0,), (0,)), ((), ())), preferred_element_type=F32)
        st_s[...] = st * chunk_decay + kv
        out = out * lax.rsqrt(jnp.mean(out * out, axis=-1, keepdims=True) + EPS) * g_ref[...]
        o_ref[pl.ds(r0, C), :] = out
        return 0
    lax.fori_loop(0, S // C, body, 0)


def _retention(proj, ret_norm):
    B, S, _ = proj.shape
    C = RET_CHUNK
    pos = jnp.arange(S, dtype=F32)
    inv_freq = 1.0 / (RET_THETA ** jnp.linspace(0.0, 1.0, DK_B // 2, dtype=F32))
    ang = pos[:, None] * inv_freq[None, :]
    cos2 = jnp.concatenate([jnp.cos(ang), jnp.cos(ang)], axis=-1)
    sin2 = jnp.concatenate([-jnp.sin(ang), jnp.sin(ang)], axis=-1)
    log_gamma = jnp.log1p(-jnp.exp2(-5.0 - jnp.arange(H_B, dtype=F32)))
    kern = functools.partial(_ret_kernel, S=S, C=C)
    return pl.pallas_call(
        kern,
        grid_spec=pltpu.PrefetchScalarGridSpec(
            num_scalar_prefetch=1,
            grid=(B, H_B),
            in_specs=[
                pl.BlockSpec((None, S, DK_B), lambda b, h, lg: (b, 0, AB_QB // DK_B + h)),
                pl.BlockSpec((None, S, DK_B), lambda b, h, lg: (b, 0, AB_KB // DK_B + h)),
                pl.BlockSpec((None, S, DV_B), lambda b, h, lg: (b, 0, AB_VB // DV_B + h)),
                pl.BlockSpec((S, DK_B), lambda b, h, lg: (0, 0)),
                pl.BlockSpec((S, DK_B), lambda b, h, lg: (0, 0)),
                pl.BlockSpec((None, 1, DV_B), lambda b, h, lg: (h, 0, 0)),
            ],
            out_specs=pl.BlockSpec((None, S, DV_B), lambda b, h, lg: (b, 0, h)),
            scratch_shapes=[pltpu.VMEM((DK_B, DV_B), F32)],
        ),
        out_shape=jax.ShapeDtypeStruct((B, S, MIX_B), F32),
        compiler_params=pltpu.CompilerParams(
            dimension_semantics=("parallel", "arbitrary"), vmem_limit_bytes=VMEM_LIMIT),
        name="retention",
    )(log_gamma, proj, proj, proj, cos2, sin2, ret_norm.reshape(H_B, 1, DV_B))


def _ab_out_kernel(a_ref, b_ref, gate_ref, x_ref, w_ref, o_ref, y_s):
    @pl.when(pl.program_id(1) == 0)
    def _():
        y_s[:, :MIX_A] = (a_ref[...] * _silu(gate_ref[:, :MIX_A])).astype(BF16)
        y_s[:, MIX_A:] = (b_ref[...] * _silu(gate_ref[:, MIX_A:])).astype(BF16)

    o_ref[...] = x_ref[...] + jnp.dot(y_s[...], w_ref[...], preferred_element_type=F32)


def _ab_out(a2d, b2d, proj2d, x2d, w, tm=512, tn=1024):
    T, D = x2d.shape
    return pl.pallas_call(
        _ab_out_kernel,
        grid=(T // tm, D // tn),
        in_specs=[
            pl.BlockSpec((tm, MIX_A), lambda i, j: (i, 0)),
            pl.BlockSpec((tm, MIX_B), lambda i, j: (i, 0)),
            pl.BlockSpec((tm, MIX_AB), lambda i, j: (i, AB_GATE // MIX_AB)),
            pl.BlockSpec((tm, tn), lambda i, j: (i, j)),
            pl.BlockSpec((MIX_AB, tn), lambda i, j: (0, j)),
        ],
        out_specs=pl.BlockSpec((tm, tn), lambda i, j: (i, j)),
        out_shape=jax.ShapeDtypeStruct((T, D), F32),
        scratch_shapes=[pltpu.VMEM((tm, MIX_AB), BF16)],
        compiler_params=pltpu.CompilerParams(
            dimension_semantics=("parallel", "arbitrary"), vmem_limit_bytes=VMEM_LIMIT),
        name="ab_out_proj",
    )(a2d, b2d, proj2d, x2d, w)


def _gdn_kernel(q_ref, k_ref, v_ref, z_ref, ba_ref, cwq_ref, cwk_ref, cwv_ref, alog_ref, dt_ref, on_ref,
                o_ref, qn_s, kn_s, va_s, bb_s, gb_s, gT_s, *, S, C):
    g = pl.program_id(1)
    NC = S // C
    REP = H_V // H_QK
    NT = (((1,), (1,)), ((), ()))
    TN = (((0,), (0,)), ((), ()))

    def conv_silu(x, w):
        rows = lax.broadcasted_iota(I32, x.shape, 0)
        y = x * w[CONV_K - 1:CONV_K, :]
        for d in range(1, CONV_K):
            xs = jnp.where(rows >= d, pltpu.roll(x, d, axis=0), 0.0)
            y = y + xs * w[CONV_K - 1 - d:CONV_K - d, :]
        return _silu(y)

    def l2n(x):
        return x * lax.rsqrt(jnp.sum(x * x, axis=-1, keepdims=True) + EPS)

    qn_s[...] = l2n(conv_silu(q_ref[...], cwq_ref[...])) * (DK_C ** -0.5)
    kn_s[...] = l2n(conv_silu(k_ref[...], cwk_ref[...]))
    va_s[...] = conv_silu(v_ref[...], cwv_ref[...])

    ba = ba_ref[...]
    beta_all = _sigmoid(ba)
    xg = ba + dt_ref[...]
    softplus = jnp.maximum(xg, 0.0) + jnp.log(1.0 + jnp.exp(-jnp.abs(xg)))
    g_all = -jnp.exp(alog_ref[...]) * softplus
    rows = lax.broadcasted_iota(I32, (S, LANES), 0)
    rc = rows & (C - 1)
    sh = 1
    while sh < C:
        g_all = g_all + jnp.where(rc >= sh, pltpu.roll(g_all, sh, axis=0), 0.0)
        sh *= 2
    lane = lax.broadcasted_iota(I32, (S, LANES), 1)
    for j in range(REP):
        hv = g * REP + j
        bcol = jnp.sum(jnp.where(lane == hv, beta_all, 0.0), axis=-1, keepdims=True)
        gcol = jnp.sum(jnp.where(lane == hv + H_V, g_all, 0.0), axis=-1, keepdims=True)
        bb_s[j] = jnp.broadcast_to(bcol, (S, LANES))
        gb_s[j] = jnp.broadcast_to(gcol, (S, LANES))

    def tr_body(c, _):
        r0 = pl.multiple_of(c * C, C)
        blk = jnp.concatenate([gb_s[j, pl.ds(r0, C), :] for j in range(REP)], axis=0)
        gT_s[c] = blk.T
        return 0
    lax.fori_loop(0, NC, tr_body, 0)

    ii = lax.broadcasted_iota(I32, (C, C), 0)
    jj = lax.broadcasted_iota(I32, (C, C), 1)
    tri = ii >= jj
    strict = ii > jj
    eye = jnp.where(ii == jj, 1.0, 0.0)

    def chunk_body(c, states):
        r0 = pl.multiple_of(c * C, C)
        qc = qn_s[pl.ds(r0, C), :]
        kc = kn_s[pl.ds(r0, C), :]
        qkk = lax.dot_general(jnp.concatenate([qc, kc], axis=0), kc, NT, preferred_element_type=F32)
        qk, kk = qkk[:C], qkk[C:]
        new_states = []
        for j in range(REP):
            st = states[j]
            gcb = gb_s[j, pl.ds(r0, C), :]
            bb = bb_s[j, pl.ds(r0, C), :]
            gcr = gT_s[c, 0:1, j * C:(j + 1) * C]
            decay = jnp.exp(jnp.where(tri, gcb[:, :C] - gcr, -jnp.inf))
            p = jnp.where(strict, -(kk * bb[:, :C] * decay), 0.0)
            attn = qk * decay
            qm = jnp.dot(p, p, preferred_element_type=F32)
            sm = eye + p
            for _ in range(4):
                r = jnp.dot(qm, jnp.concatenate([qm, sm], axis=1), preferred_element_type=F32)
                qm, sm = r[:, :C], sm + r[:, C:]
            tinv = sm + jnp.dot(qm, sm, preferred_element_type=F32)
            vb = va_s[pl.ds(r0, C), j * DV_C:(j + 1) * DV_C] * bb
            kbg = kc * bb * jnp.exp(gcb)
            uw = jnp.dot(tinv, jnp.concatenate([vb, kbg], axis=1), preferred_element_type=F32)
            glast = gcb[C - 1:C, :]
            kt = kc * jnp.exp(glast - gcb)
            qd = qc * jnp.exp(gcb)
            ktuw = lax.dot_general(kt, uw, TN, preferred_element_type=F32)
            atuw = jnp.dot(attn, uw, preferred_element_type=F32)
            lhs = jnp.concatenate([ktuw[:, DV_C:], qd - atuw[:, DV_C:]], axis=0)
            r = jnp.dot(lhs, st, preferred_element_type=F32)
            o = r[DK_C:] + atuw[:, :DV_C]
            new_states.append(st * jnp.exp(glast) + ktuw[:, :DV_C] - r[:DK_C])
            o = o * lax.rsqrt(jnp.mean(o * o, axis=-1, keepdims=True) + EPS) * on_ref[...]
            o = o * _silu(z_ref[pl.ds(r0, C), j * DV_C:(j + 1) * DV_C])
            o_ref[pl.ds(r0, C), j * DV_C:(j + 1) * DV_C] = o.astype(o_ref.dtype)
        return tuple(new_states)

    lax.fori_loop(0, NC, chunk_body, tuple(jnp.zeros((DK_C, DV_C), F32) for _ in range(REP)))


def _gdn(proj, conv_w, a_log, dt_bias, out_norm):
    B, S, _ = proj.shape
    C = DN_CHUNK
    REP = H_V // H_QK
    VW = REP * DV_C
    alog_row = jnp.zeros((1, LANES), F32).at[0, H_V:2 * H_V].set(a_log)
    dt_row = jnp.zeros((1, LANES), F32).at[0, H_V:2 * H_V].set(dt_bias)
    kern = functools.partial(_gdn_kernel, S=S, C=C)
    return pl.pallas_call(
        kern,
        grid=(B, H_QK),
        in_specs=[
            pl.BlockSpec((None, S, DK_C), lambda b, g: (b, 0, DN_Q // DK_C + g)),
            pl.BlockSpec((None, S, DK_C), lambda b, g: (b, 0, DN_K // DK_C + g)),
            pl.BlockSpec((None, S, VW), lambda b, g: (b, 0, DN_V // VW + g)),
            pl.BlockSpec((None, S, VW), lambda b, g: (b, 0, DN_Z // VW + g)),
            pl.BlockSpec((None, S, LANES), lambda b, g: (b, 0, DN_BA // LANES)),
            pl.BlockSpec((CONV_K, DK_C), lambda b, g: (0, g)),
            pl.BlockSpec((CONV_K, DK_C), lambda b, g: (0, H_QK + g)),
            pl.BlockSpec((CONV_K, VW), lambda b, g: (0, H_QK + g)),
            pl.BlockSpec((1, LANES), lambda b, g: (0, 0)),
            pl.BlockSpec((1, LANES), lambda b, g: (0, 0)),
            pl.BlockSpec((1, DV_C), lambda b, g: (0, 0)),
        ],
        out_specs=pl.BlockSpec((None, S, VW), lambda b, g: (b, 0, g)),
        out_shape=jax.ShapeDtypeStruct((B, S, MIX_C), BF16),
        scratch_shapes=[
            pltpu.VMEM((S, DK_C), F32),
            pltpu.VMEM((S, DK_C), F32),
            pltpu.VMEM((S, VW), F32),
            pltpu.VMEM((REP, S, LANES), F32),
            pltpu.VMEM((REP, S, LANES), F32),
            pltpu.VMEM((S // C, LANES, REP * C), F32),
        ],
        compiler_params=pltpu.CompilerParams(
            dimension_semantics=("parallel", "arbitrary"), vmem_limit_bytes=VMEM_LIMIT),
        name="gated_deltanet",
    )(proj, proj, proj, proj, proj, conv_w, conv_w, conv_w, alog_row, dt_row, out_norm.reshape(1, DV_C))


def _dn_out_kernel(o_in_ref, x_ref, w_ref, o_ref):
    o_ref[...] = x_ref[...] + jnp.dot(o_in_ref[...], w_ref[...], preferred_element_type=F32)


def _dn_out(o2d, x2d, w, tm=512, tn=1024):
    T, D = x2d.shape
    K = o2d.shape[1]
    return pl.pallas_call(
        _dn_out_kernel,
        grid=(T // tm, D // tn),
        in_specs=[
            pl.BlockSpec((tm, K), lambda i, j: (i, 0)),
            pl.BlockSpec((tm, tn), lambda i, j: (i, j)),
            pl.BlockSpec((K, tn), lambda i, j: (0, j)),
        ],
        out_specs=pl.BlockSpec((tm, tn), lambda i, j: (i, j)),
        out_shape=jax.ShapeDtypeStruct((T, D), F32),
        compiler_params=pltpu.CompilerParams(
            dimension_semantics=("parallel", "arbitrary"), vmem_limit_bytes=VMEM_LIMIT),
        name="dn_out_proj",
    )(o2d, x2d, w)


def _ab_weight(w):
    D = w.shape[0]
    o = np.cumsum([0, MIX_A, D_C, H_I * D_I, D_I, H_I, H_B * DK_B, H_B * DK_B, MIX_B, MIX_AB])
    q_a, c, q_i, k_i, w_i, q_b, k_b, v_b, gate = [w[:, o[i]:o[i + 1]] for i in range(9)]
    pad_kw = jnp.zeros((D, LANES - D_I - H_I), w.dtype)
    pad_end = jnp.zeros((D, AB_NP - AB_KW - LANES), w.dtype)
    return jnp.concatenate([q_a, v_b, gate, q_i, q_b, k_b, c, k_i, w_i, pad_kw, pad_end], axis=1).astype(BF16)


def _dn_weight(w):
    D = w.shape[0]
    conv_ch = 2 * H_QK * DK_C + MIX_C
    main = w[:, :conv_ch + MIX_C]
    ba = w[:, conv_ch + MIX_C:]
    pad = jnp.zeros((D, DN_NP - DN_BA - 2 * H_V), w.dtype)
    return jnp.concatenate([main, ba, pad], axis=1).astype(BF16)


def kernel(x, ab_norm, ab_w_in, ab_kv_norm, ab_w_uk, ab_w_uv, ab_q_norm, ab_k_norm, ab_ret_norm, ab_w_out,
           dn_norm, dn_w_in, dn_conv, dn_a_log, dn_dt_bias, dn_out_norm, dn_w_out):
    B, S, D = x.shape
    T = B * S
    depth = ab_norm.shape[0] + dn_norm.shape[0]
    h = x.astype(F32).reshape(T, D)
    for layer in range(depth):
        j = layer // 2
        if layer % 2 == 0:
            proj = _norm_proj(h, ab_norm[j].reshape(1, D), _ab_weight(ab_w_in[j]), tm=1024, tn=1024)
            proj3 = proj.reshape(B, S, AB_NP)
            a_out = _dsa(proj3, ab_kv_norm[j].reshape(1, D_C), ab_w_uk[j].astype(BF16),
                         ab_k_norm[j].reshape(1, DH_A), ab_q_norm[j].reshape(1, DH_A), ab_w_uv[j].astype(BF16))
            b_out = _retention(proj3, ab_ret_norm[j])
            h = _ab_out(a_out.reshape(T, MIX_A), b_out.reshape(T, MIX_B), proj, h, ab_w_out[j].astype(BF16))
        else:
            proj = _norm_proj(h, dn_norm[j].reshape(1, D), _dn_weight(dn_w_in[j]), tm=1024, tn=512)
            o = _gdn(proj.reshape(B, S, DN_NP), dn_conv[j], dn_a_log[j], dn_dt_bias[j], dn_out_norm[j])
            h = _dn_out(o.reshape(T, MIX_C), h, dn_w_out[j].astype(BF16))
    return h.reshape(B, S, D).astype(x.dtype)
```

```python
import functools

import numpy as np
import jax
import jax.numpy as jnp
from jax import lax
from jax.experimental import pallas as pl
from jax.experimental.pallas import tpu as pltpu

F32 = jnp.float32
BF16 = jnp.bfloat16
I32 = jnp.int32

EPS = 1e-6
H_A, DH_A, D_C, H_I, D_I = 8, 128, 256, 8, 64
INDEX_TOPK = 256
H_B, DK_B, DV_B, RET_CHUNK, RET_THETA = 4, 128, 256, 128, 10000.0
H_QK, H_V, DK_C, DV_C, CONV_K, DN_CHUNK = 16, 32, 128, 128, 4, 64

MIX_A = H_A * DH_A
MIX_B = H_B * DV_B
MIX_AB = MIX_A + MIX_B
MIX_C = H_V * DV_C

LANES = 128
INT_MIN = np.int32(-2**31)

AB_QA, AB_VB, AB_GATE, AB_QI, AB_QB, AB_KB, AB_C, AB_KW, AB_NP = 0, 1024, 2048, 4096, 4608, 5120, 5632, 5888, 6144
DN_Q, DN_K, DN_V, DN_Z, DN_BA, DN_NP = 0, 2048, 4096, 8192, 12288, 12800

VMEM_LIMIT = 56 * 1024 * 1024


def _sigmoid(x):
    return 1.0 / (1.0 + jnp.exp(-x))


def _silu(x):
    return x * _sigmoid(x)


def _norm_proj_kernel(x_ref, g_ref, w_ref, o_ref, xn_ref):
    @pl.when(pl.program_id(1) == 0)
    def _():
        x = x_ref[...]
        ms = jnp.mean(x * x, axis=-1, keepdims=True)
        xn_ref[...] = (x * lax.rsqrt(ms + EPS) * g_ref[...]).astype(BF16)

    o_ref[...] = jnp.dot(xn_ref[...], w_ref[...], preferred_element_type=F32)


def _norm_proj(x2d, gain, w, tm, tn):
    T, D = x2d.shape
    N = w.shape[1]
    return pl.pallas_call(
        _norm_proj_kernel,
        grid=(T // tm, N // tn),
        in_specs=[
            pl.BlockSpec((tm, D), lambda i, j: (i, 0)),
            pl.BlockSpec((1, D), lambda i, j: (0, 0)),
            pl.BlockSpec((D, tn), lambda i, j: (0, j)),
        ],
        out_specs=pl.BlockSpec((tm, tn), lambda i, j: (i, j)),
        out_shape=jax.ShapeDtypeStruct((T, N), F32),
        scratch_shapes=[pltpu.VMEM((tm, D), BF16)],
        compiler_params=pltpu.CompilerParams(
            dimension_semantics=("parallel", "arbitrary"), vmem_limit_bytes=VMEM_LIMIT),
        name="norm_proj",
    )(x2d, gain, w)


def _dsa_kernel(qa_ref, c_ref, qi_ref, kwq_ref, kwk_ref, kvn_ref, wuk_ref, kn_ref, qn_ref, wuv_ref,
                o_ref, cnT_s, k_s, ki_s, key_s, bias_s, lg_s, j_s, qall_s, lat_s, *, S, Q, topk):
    qb = pl.program_id(1)
    nkb = qb + 1
    RB = 128
    nrb = nkb * (Q // RB)
    NT = (((1,), (1,)), ((), ()))

    @pl.when(qb == 0)
    def _prep():
        def body(j, _):
            r0 = pl.multiple_of(j * Q, Q)
            c = c_ref[pl.ds(r0, Q), :]
            cn = c * lax.rsqrt(jnp.mean(c * c, axis=-1, keepdims=True) + EPS) * kvn_ref[...]
            kk = jnp.dot(cn.astype(BF16), wuk_ref[...], preferred_element_type=F32)
            kk = kk * lax.rsqrt(jnp.mean(kk * kk, axis=-1, keepdims=True) + EPS) * kn_ref[...]
            cnT_s[j] = cn.T.astype(BF16)
            k_s[pl.ds(r0, Q), :] = kk.astype(BF16)
            ki_s[pl.ds(r0, Q), :] = kwk_ref[pl.ds(r0, Q), 0:D_I].astype(BF16)
            return 0
        lax.fori_loop(0, S // Q, body, 0)

    tpos = qb * Q + lax.broadcasted_iota(I32, (1, Q), 1)
    w_t = kwq_ref[...].T[D_I:D_I + H_I, :] * (H_I ** -0.5 * D_I ** -0.5)

    def score_blk(j, _):
        r0 = pl.multiple_of(j * Q, Q)
        kib = ki_s[pl.ds(r0, Q), :]
        acc = jnp.zeros((Q, Q), F32)
        for h in range(H_I):
            qih = qi_ref[:, h * D_I:(h + 1) * D_I].astype(BF16)
            s = lax.dot_general(kib, qih, NT, preferred_element_type=F32)
            acc = acc + jnp.maximum(s, 0.0) * w_t[h:h + 1, :]
        acc = acc + 0.0
        bits = pltpu.bitcast(acc, I32)
        key = bits ^ ((bits >> 31) & np.int32(0x7FFFFFFF))
        kpos = r0 + lax.broadcasted_iota(I32, (Q, Q), 0)
        key_s[pl.ds(r0, Q), :] = jnp.where(kpos <= tpos, key, INT_MIN)
        return 0
    lax.fori_loop(0, nkb, score_blk, 0)

    def count(pred):
        def body(i, acc):
            r0 = pl.multiple_of(i * RB, RB)
            kpos = r0 + lax.broadcasted_iota(I32, (RB, Q), 0)
            return acc + jnp.where(pred(key_s[pl.ds(r0, RB), :], kpos), 1.0, 0.0)
        acc = lax.fori_loop(0, nrb, body, jnp.zeros((RB, Q), F32))
        return jnp.sum(acc, axis=0, keepdims=True)

    def bit_body(i, prefix_u):
        cand_u = prefix_u | jnp.left_shift(np.int32(1), 31 - i)
        cand_s = cand_u ^ INT_MIN
        cnt = count(lambda k, _: k >= cand_s)
        return jnp.where(cnt >= float(topk), cand_u, prefix_u)
    thr = lax.fori_loop(0, 32, bit_body, jnp.zeros((1, Q), I32)) ^ INT_MIN

    cnt_ge = count(lambda k, _: k >= thr)
    ambiguous = (cnt_ge != float(topk)) & (tpos >= topk - 1)
    j_s[...] = jnp.full((1, Q), S, I32)

    @pl.when(jnp.max(jnp.where(ambiguous, 1.0, 0.0)) > 0.0)
    def _ties():
        need = float(topk) - count(lambda k, _: k > thr)
        nbits = int(S).bit_length() - 1

        def jbit(i, prefix):
            cand = prefix | jnp.left_shift(np.int32(1), nbits - 1 - i)
            cnt = count(lambda k, kpos: (k == thr) & (kpos < cand))
            return jnp.where(cnt < need, cand, prefix)
        jstar = lax.fori_loop(0, nbits, jbit, jnp.zeros((1, Q), I32))
        j_s[...] = jnp.where(ambiguous, jstar, S)

    jsel = j_s[...]

    def bias_blk(i, _):
        r0 = pl.multiple_of(i * RB, RB)
        k = key_s[pl.ds(r0, RB), :]
        kpos = r0 + lax.broadcasted_iota(I32, (RB, Q), 0)
        sel = (kpos <= tpos) & ((k > thr) | ((k == thr) & (kpos <= jsel)))
        bias_s[pl.ds(r0, RB), :] = jnp.where(sel, 0.0, -jnp.inf)
        return 0
    lax.fori_loop(0, nrb, bias_blk, 0)

    for h in range(H_A):
        qh = qa_ref[:, h * DH_A:(h + 1) * DH_A]
        qh = qh * lax.rsqrt(jnp.mean(qh * qh, axis=-1, keepdims=True) + EPS) * qn_ref[...] * (DH_A ** -0.5)
        qall_s[h * Q:(h + 1) * Q, :] = qh.astype(BF16)

    def pass_a(j, m):
        r0 = pl.multiple_of(j * Q, Q)
        s = lax.dot_general(k_s[pl.ds(r0, Q), :], qall_s[...], NT, preferred_element_type=F32)
        bias = bias_s[pl.ds(r0, Q), :]
        ms = []
        for h in range(H_A):
            sh = s[:, h * Q:(h + 1) * Q] + bias
            lg_s[pl.ds(r0, Q), h * Q:(h + 1) * Q] = sh
            ms.append(jnp.max(sh, axis=0, keepdims=True))
        return jnp.maximum(m, jnp.concatenate(ms, axis=1))
    m = lax.fori_loop(0, nkb, pass_a, jnp.full((1, H_A * Q), -jnp.inf, F32))

    lat_s[...] = jnp.zeros_like(lat_s)

    def pass_b(j, l):
        r0 = pl.multiple_of(j * Q, Q)
        p = jnp.exp(lg_s[pl.ds(r0, Q), :] - m)
        lat_s[...] += jnp.dot(cnT_s[j], p.astype(BF16), preferred_element_type=F32)
        return l + jnp.sum(p, axis=0, keepdims=True)
    l = lax.fori_loop(0, nkb, pass_b, jnp.zeros((1, H_A * Q), F32))

    lat = (lat_s[...] * (1.0 / l)).astype(BF16)
    for h in range(H_A):
        o_ref[:, h * DH_A:(h + 1) * DH_A] = lax.dot_general(
            lat[:, h * Q:(h + 1) * Q], wuv_ref[h], (((0,), (0,)), ((), ())), preferred_element_type=F32)


def _dsa(proj, kv_norm, w_uk, k_norm, q_norm, w_uv, Q=256):
    B, S, _ = proj.shape
    topk = min(INDEX_TOPK, S // 4)
    assert S % Q == 0 and topk <= Q and (S & (S - 1)) == 0
    kern = functools.partial(_dsa_kernel, S=S, Q=Q, topk=topk)
    full = lambda *shape: pl.BlockSpec(shape, lambda b, q: (0,) * len(shape))
    return pl.pallas_call(
        kern,
        grid=(B, S // Q),
        in_specs=[
            pl.BlockSpec((None, Q, MIX_A), lambda b, q: (b, q, AB_QA // MIX_A)),
            pl.BlockSpec((None, S, D_C), lambda b, q: (b, 0, AB_C // D_C)),
            pl.BlockSpec((None, Q, H_I * D_I), lambda b, q: (b, q, AB_QI // (H_I * D_I))),
            pl.BlockSpec((None, Q, LANES), lambda b, q: (b, q, AB_KW // LANES)),
            pl.BlockSpec((None, S, LANES), lambda b, q: (b, 0, AB_KW // LANES)),
            full(1, D_C), full(D_C, DH_A), full(1, DH_A), full(1, DH_A), full(H_A, D_C, DH_A),
        ],
        out_specs=pl.BlockSpec((None, Q, MIX_A), lambda b, q: (b, q, 0)),
        out_shape=jax.ShapeDtypeStruct((B, S, MIX_A), F32),
        scratch_shapes=[
            pltpu.VMEM((S // Q, D_C, Q), BF16),
            pltpu.VMEM((S, DH_A), BF16),
            pltpu.VMEM((S, D_I), BF16),
            pltpu.VMEM((S, Q), I32),
            pltpu.VMEM((S, Q), F32),
            pltpu.VMEM((S, H_A * Q), F32),
            pltpu.VMEM((1, Q), I32),
            pltpu.VMEM((H_A * Q, DH_A), BF16),
            pltpu.VMEM((D_C, H_A * Q), F32),
        ],
        compiler_params=pltpu.CompilerParams(
            dimension_semantics=("parallel", "arbitrary"), vmem_limit_bytes=VMEM_LIMIT),
        name="dsa_attention",
    )(proj, proj, proj, proj, proj, kv_norm, w_uk, k_norm, q_norm, w_uv)


def _ret_kernel(lg_ref, q_ref, k_ref, v_ref, cos_ref, sin_ref, g_ref, o_ref, st_s, *, S, C):
    h = pl.program_id(1)
    lg = lg_ref[h]
    NT = (((1,), (1,)), ((), ()))
    ii = lax.broadcasted_iota(I32, (C, C), 0)
    jj = lax.broadcasted_iota(I32, (C, C), 1)
    diff = (ii - jj).astype(F32)
    dmask = jnp.where(diff >= 0, jnp.exp(jnp.maximum(diff, 0.0) * lg), 0.0)
    icol = lax.broadcasted_iota(I32, (C, 1), 0).astype(F32)
    k_dec = jnp.exp((C - 1 - icol) * lg)
    q_dec = jnp.exp((icol + 1.0) * lg)
    chunk_decay = jnp.exp(jnp.full((1, DV_B), float(C), F32) * lg)
    st_s[...] = jnp.zeros_like(st_s)

    def rot(x, cos, sin):
        return x * cos + pltpu.roll(x, DK_B // 2, axis=1) * sin

    def body(n, _):
        r0 = pl.multiple_of(n * C, C)
        cos = cos_ref[pl.ds(r0, C), :]
        sin = sin_ref[pl.ds(r0, C), :]
        q = rot(q_ref[pl.ds(r0, C), :], cos, sin)
        k = rot(k_ref[pl.ds(r0, C), :], cos, sin) * (DK_B ** -0.5)
        v = v_ref[pl.ds(r0, C), :]
        st = st_s[...]
        inner = lax.dot_general(q, k, NT, preferred_element_type=F32) * dmask
        out = jnp.dot(inner, v, preferred_element_type=F32)
        out = out + jnp.dot(q * q_dec, st, preferred_element_type=F32)
        kv = lax.dot_general(k * k_dec, v, (((0,), (0,)), ((), ())), preferred_element_type=F32)
        st_s[...] = st * chunk_decay + kv
        out = out * lax.rsqrt(jnp.mean(out * out, axis=-1, keepdims=True) + EPS) * g_ref[...]
        o_ref[pl.ds(r0, C), :] = out
        return 0
    lax.fori_loop(0, S // C, body, 0)


def _retention(proj, ret_norm):
    B, S, _ = proj.shape
    C = RET_CHUNK
    pos = jnp.arange(S, dtype=F32)
    inv_freq = 1.0 / (RET_THETA ** jnp.linspace(0.0, 1.0, DK_B // 2, dtype=F32))
    ang = pos[:, None] * inv_freq[None, :]
    cos2 = jnp.concatenate([jnp.cos(ang), jnp.cos(ang)], axis=-1)
    sin2 = jnp.concatenate([-jnp.sin(ang), jnp.sin(ang)], axis=-1)
    log_gamma = jnp.log1p(-jnp.exp2(-5.0 - jnp.arange(H_B, dtype=F32)))
    kern = functools.partial(_ret_kernel, S=S, C=C)
    return pl.pallas_call(
        kern,
        grid_spec=pltpu.PrefetchScalarGridSpec(
            num_scalar_prefetch=1,
            grid=(B, H_B),
            in_specs=[
                pl.BlockSpec((None, S, DK_B), lambda b, h, lg: (b, 0, AB_QB // DK_B + h)),
                pl.BlockSpec((None, S, DK_B), lambda b, h, lg: (b, 0, AB_KB // DK_B + h)),
                pl.BlockSpec((None, S, DV_B), lambda b, h, lg: (b, 0, AB_VB // DV_B + h)),
                pl.BlockSpec((S, DK_B), lambda b, h, lg: (0, 0)),
                pl.BlockSpec((S, DK_B), lambda b, h, lg: (0, 0)),
                pl.BlockSpec((None, 1, DV_B), lambda b, h, lg: (h, 0, 0)),
            ],
            out_specs=pl.BlockSpec((None, S, DV_B), lambda b, h, lg: (b, 0, h)),
            scratch_shapes=[pltpu.VMEM((DK_B, DV_B), F32)],
        ),
        out_shape=jax.ShapeDtypeStruct((B, S, MIX_B), F32),
        compiler_params=pltpu.CompilerParams(
            dimension_semantics=("parallel", "arbitrary"), vmem_limit_bytes=VMEM_LIMIT),
        name="retention",
    )(log_gamma, proj, proj, proj, cos2, sin2, ret_norm.reshape(H_B, 1, DV_B))


def _ab_out_kernel(a_ref, b_ref, gate_ref, x_ref, w_ref, o_ref, y_s):
    @pl.when(pl.program_id(1) == 0)
    def _():
        y_s[:, :MIX_A] = (a_ref[...] * _silu(gate_ref[:, :MIX_A])).astype(BF16)
        y_s[:, MIX_A:] = (b_ref[...] * _silu(gate_ref[:, MIX_A:])).astype(BF16)

    o_ref[...] = x_ref[...] + jnp.dot(y_s[...], w_ref[...], preferred_element_type=F32)


def _ab_out(a2d, b2d, proj2d, x2d, w, tm=512, tn=1024):
    T, D = x2d.shape
    return pl.pallas_call(
        _ab_out_kernel,
        grid=(T // tm, D // tn),
        in_specs=[
            pl.BlockSpec((tm, MIX_A), lambda i, j: (i, 0)),
            pl.BlockSpec((tm, MIX_B), lambda i, j: (i, 0)),
            pl.BlockSpec((tm, MIX_AB), lambda i, j: (i, AB_GATE // MIX_AB)),
            pl.BlockSpec((tm, tn), lambda i, j: (i, j)),
            pl.BlockSpec((MIX_AB, tn), lambda i, j: (0, j)),
        ],
        out_specs=pl.BlockSpec((tm, tn), lambda i, j: (i, j)),
        out_shape=jax.ShapeDtypeStruct((T, D), F32),
        scratch_shapes=[pltpu.VMEM((tm, MIX_AB), BF16)],
        compiler_params=pltpu.CompilerParams(
            dimension_semantics=("parallel", "arbitrary"), vmem_limit_bytes=VMEM_LIMIT),
        name="ab_out_proj",
    )(a2d, b2d, proj2d, x2d, w)


def _gdn_kernel(q_ref, k_ref, v_ref, z_ref, ba_ref, cwq_ref, cwk_ref, cwv_ref, alog_ref, dt_ref, on_ref,
                o_ref, qn_s, kn_s, va_s, bb_s, gb_s, *, S, C):
    g = pl.program_id(1)
    NC = S // C
    REP = H_V // H_QK
    NT = (((1,), (1,)), ((), ()))
    TN = (((0,), (0,)), ((), ()))

    def conv_silu(x, w):
        rows = lax.broadcasted_iota(I32, x.shape, 0)
        y = x * w[CONV_K - 1:CONV_K, :]
        for d in range(1, CONV_K):
            xs = jnp.where(rows >= d, pltpu.roll(x, d, axis=0), 0.0)
            y = y + xs * w[CONV_K - 1 - d:CONV_K - d, :]
        return _silu(y)

    def l2n(x):
        return x * lax.rsqrt(jnp.sum(x * x, axis=-1, keepdims=True) + EPS)

    qn_s[...] = l2n(conv_silu(q_ref[...], cwq_ref[...])) * (DK_C ** -0.5)
    kn_s[...] = l2n(conv_silu(k_ref[...], cwk_ref[...]))
    va_s[...] = conv_silu(v_ref[...], cwv_ref[...])

    ba = ba_ref[...]
    beta_all = _sigmoid(ba)
    xg = ba + dt_ref[...]
    softplus = jnp.maximum(xg, 0.0) + jnp.log(1.0 + jnp.exp(-jnp.abs(xg)))
    g_all = -jnp.exp(alog_ref[...]) * softplus
    rows = lax.broadcasted_iota(I32, (S, LANES), 0)
    rc = rows & (C - 1)
    sh = 1
    while sh < C:
        g_all = g_all + jnp.where(rc >= sh, pltpu.roll(g_all, sh, axis=0), 0.0)
        sh *= 2
    lane = lax.broadcasted_iota(I32, (S, LANES), 1)
    for j in range(REP):
        hv = g * REP + j
        bcol = jnp.sum(jnp.where(lane == hv, beta_all, 0.0), axis=-1, keepdims=True)
        gcol = jnp.sum(jnp.where(lane == hv + H_V, g_all, 0.0), axis=-1, keepdims=True)
        bb_s[j] = jnp.broadcast_to(bcol, (S, LANES))
        gb_s[j] = jnp.broadcast_to(gcol, (S, LANES))

    ii = lax.broadcasted_iota(I32, (C, C), 0)
    jj = lax.broadcasted_iota(I32, (C, C), 1)
    tri = ii >= jj
    strict = ii > jj
    eye = jnp.where(ii == jj, 1.0, 0.0)

    GC = 8
    dot = functools.partial(jnp.dot, preferred_element_type=F32)

    def group_body(gi, states):
        chains = [(c, j) for c in range(GC) for j in range(REP)]
        r0s = [pl.multiple_of((gi * GC + c) * C, C) for c in range(GC)]
        qc = [qn_s[pl.ds(r0s[c], C), :] for c in range(GC)]
        kc = [kn_s[pl.ds(r0s[c], C), :] for c in range(GC)]
        qkk = [lax.dot_general(jnp.concatenate([qc[c], kc[c]], axis=0), kc[c], NT, preferred_element_type=F32)
               for c in range(GC)]
        gcb, bb, p, attn = {}, {}, {}, {}
        for c, j in chains:
            gcb[c, j] = gb_s[j, pl.ds(r0s[c], C), :]
            bb[c, j] = bb_s[j, pl.ds(r0s[c], C), :]
            gcr = jnp.sum(jnp.where(ii == jj, gcb[c, j][:, :C], 0.0), axis=0, keepdims=True)
            decay = jnp.exp(jnp.where(tri, gcb[c, j][:, :C] - gcr, -jnp.inf))
            p[c, j] = jnp.where(strict, -(qkk[c][C:] * bb[c, j][:, :C] * decay), 0.0)
            attn[c, j] = qkk[c][:C] * decay
        qm = {k: dot(p[k], p[k]) for k in chains}
        sm = {k: eye + p[k] for k in chains}
        for _ in range(4):
            sm = {k: sm[k] + dot(qm[k], sm[k]) for k in chains}
            qm = {k: dot(qm[k], qm[k]) for k in chains}
        tinv = {k: sm[k] + dot(qm[k], sm[k]) for k in chains}
        uw, glast = {}, {}
        for c, j in chains:
            vb = va_s[pl.ds(r0s[c], C), j * DV_C:(j + 1) * DV_C] * bb[c, j]
            kbg = kc[c] * bb[c, j] * jnp.exp(gcb[c, j])
            uw[c, j] = dot(tinv[c, j], jnp.concatenate([vb, kbg], axis=1))
            glast[c, j] = gcb[c, j][C - 1:C, :]
        ktuw = {(c, j): lax.dot_general(kc[c] * jnp.exp(glast[c, j] - gcb[c, j]), uw[c, j], TN,
                                        preferred_element_type=F32) for c, j in chains}
        atuw = {k: dot(attn[k], uw[k]) for k in chains}
        lhs = {(c, j): jnp.concatenate([ktuw[c, j][:, DV_C:], qc[c] * jnp.exp(gcb[c, j]) - atuw[c, j][:, DV_C:]],
                                       axis=0) for c, j in chains}
        states = list(states)
        for c in range(GC):
            for j in range(REP):
                r = dot(lhs[c, j], states[j])
                o = r[DK_C:] + atuw[c, j][:, :DV_C]
                states[j] = states[j] * jnp.exp(glast[c, j]) + ktuw[c, j][:, :DV_C] - r[:DK_C]
                o = o * lax.rsqrt(jnp.mean(o * o, axis=-1, keepdims=True) + EPS) * on_ref[...]
                o = o * _silu(z_ref[pl.ds(r0s[c], C), j * DV_C:(j + 1) * DV_C])
                o_ref[pl.ds(r0s[c], C), j * DV_C:(j + 1) * DV_C] = o.astype(o_ref.dtype)
        return tuple(states)

    lax.fori_loop(0, NC // GC, group_body, tuple(jnp.zeros((DK_C, DV_C), F32) for _ in range(REP)))


def _gdn(proj, conv_w, a_log, dt_bias, out_norm):
    B, S, _ = proj.shape
    C = DN_CHUNK
    REP = H_V // H_QK
    VW = REP * DV_C
    alog_row = jnp.zeros((1, LANES), F32).at[0, H_V:2 * H_V].set(a_log)
    dt_row = jnp.zeros((1, LANES), F32).at[0, H_V:2 * H_V].set(dt_bias)
    kern = functools.partial(_gdn_kernel, S=S, C=C)
    return pl.pallas_call(
        kern,
        grid=(B, H_QK),
        in_specs=[
            pl.BlockSpec((None, S, DK_C), lambda b, g: (b, 0, DN_Q // DK_C + g)),
            pl.BlockSpec((None, S, DK_C), lambda b, g: (b, 0, DN_K // DK_C + g)),
            pl.BlockSpec((None, S, VW), lambda b, g: (b, 0, DN_V // VW + g)),
            pl.BlockSpec((None, S, VW), lambda b, g: (b, 0, DN_Z // VW + g)),
            pl.BlockSpec((None, S, LANES), lambda b, g: (b, 0, DN_BA // LANES)),
            pl.BlockSpec((CONV_K, DK_C), lambda b, g: (0, g)),
            pl.BlockSpec((CONV_K, DK_C), lambda b, g: (0, H_QK + g)),
            pl.BlockSpec((CONV_K, VW), lambda b, g: (0, H_QK + g)),
            pl.BlockSpec((1, LANES), lambda b, g: (0, 0)),
            pl.BlockSpec((1, LANES), lambda b, g: (0, 0)),
            pl.BlockSpec((1, DV_C), lambda b, g: (0, 0)),
        ],
        out_specs=pl.BlockSpec((None, S, VW), lambda b, g: (b, 0, g)),
        out_shape=jax.ShapeDtypeStruct((B, S, MIX_C), BF16),
        scratch_shapes=[
            pltpu.VMEM((S, DK_C), F32),
            pltpu.VMEM((S, DK_C), F32),
            pltpu.VMEM((S, VW), F32),
            pltpu.VMEM((REP, S, LANES), F32),
            pltpu.VMEM((REP, S, LANES), F32),
        ],
        compiler_params=pltpu.CompilerParams(
            dimension_semantics=("parallel", "arbitrary"), vmem_limit_bytes=VMEM_LIMIT),
        name="gated_deltanet",
    )(proj, proj, proj, proj, proj, conv_w, conv_w, conv_w, alog_row, dt_row, out_norm.reshape(1, DV_C))


def _dn_out_kernel(o_in_ref, x_ref, w_ref, o_ref):
    o_ref[...] = x_ref[...] + jnp.dot(o_in_ref[...], w_ref[...], preferred_element_type=F32)


def _dn_out(o2d, x2d, w, tm=512, tn=1024):
    T, D = x2d.shape
    K = o2d.shape[1]
    return pl.pallas_call(
        _dn_out_kernel,
        grid=(T // tm, D // tn),
        in_specs=[
            pl.BlockSpec((tm, K), lambda i, j: (i, 0)),
            pl.BlockSpec((tm, tn), lambda i, j: (i, j)),
            pl.BlockSpec((K, tn), lambda i, j: (0, j)),
        ],
        out_specs=pl.BlockSpec((tm, tn), lambda i, j: (i, j)),
        out_shape=jax.ShapeDtypeStruct((T, D), F32),
        compiler_params=pltpu.CompilerParams(
            dimension_semantics=("parallel", "arbitrary"), vmem_limit_bytes=VMEM_LIMIT),
        name="dn_out_proj",
    )(o2d, x2d, w)


def _ab_weight(w):
    D = w.shape[0]
    o = np.cumsum([0, MIX_A, D_C, H_I * D_I, D_I, H_I, H_B * DK_B, H_B * DK_B, MIX_B, MIX_AB])
    q_a, c, q_i, k_i, w_i, q_b, k_b, v_b, gate = [w[:, o[i]:o[i + 1]] for i in range(9)]
    pad_kw = jnp.zeros((D, LANES - D_I - H_I), w.dtype)
    pad_end = jnp.zeros((D, AB_NP - AB_KW - LANES), w.dtype)
    return jnp.concatenate([q_a, v_b, gate, q_i, q_b, k_b, c, k_i, w_i, pad_kw, pad_end], axis=1).astype(BF16)


def _dn_weight(w):
    D = w.shape[0]
    conv_ch = 2 * H_QK * DK_C + MIX_C
    main = w[:, :conv_ch + MIX_C]
    ba = w[:, conv_ch + MIX_C:]
    pad = jnp.zeros((D, DN_NP - DN_BA - 2 * H_V), w.dtype)
    return jnp.concatenate([main, ba, pad], axis=1).astype(BF16)


def kernel(x, ab_norm, ab_w_in, ab_kv_norm, ab_w_uk, ab_w_uv, ab_q_norm, ab_k_norm, ab_ret_norm, ab_w_out,
           dn_norm, dn_w_in, dn_conv, dn_a_log, dn_dt_bias, dn_out_norm, dn_w_out):
    B, S, D = x.shape
    T = B * S
    depth = ab_norm.shape[0] + dn_norm.shape[0]
    h = x.astype(F32).reshape(T, D)
    for layer in range(depth):
        j = layer // 2
        if layer % 2 == 0:
            proj = _norm_proj(h, ab_norm[j].reshape(1, D), _ab_weight(ab_w_in[j]), tm=1024, tn=1024)
            proj3 = proj.reshape(B, S, AB_NP)
            a_out = _dsa(proj3, ab_kv_norm[j].reshape(1, D_C), ab_w_uk[j].astype(BF16),
                         ab_k_norm[j].reshape(1, DH_A), ab_q_norm[j].reshape(1, DH_A), ab_w_uv[j].astype(BF16))
            b_out = _retention(proj3, ab_ret_norm[j])
            h = _ab_out(a_out.reshape(T, MIX_A), b_out.reshape(T, MIX_B), proj, h, ab_w_out[j].astype(BF16))
        else:
            proj = _norm_proj(h, dn_norm[j].reshape(1, D), _dn_weight(dn_w_in[j]), tm=1024, tn=512)
            o = _gdn(proj.reshape(B, S, DN_NP), dn_conv[j], dn_a_log[j], dn_dt_bias[j], dn_out_norm[j])
            h = _dn_out(o.reshape(T, MIX_C), h, dn_w_out[j].astype(BF16))
    return h.reshape(B, S, D).astype(x.dtype)
```

```python
import functools

import numpy as np
import jax
import jax.numpy as jnp
from jax import lax
from jax.experimental import pallas as pl
from jax.experimental.pallas import tpu as pltpu

F32 = jnp.float32
BF16 = jnp.bfloat16
I32 = jnp.int32

EPS = 1e-6
H_A, DH_A, D_C, H_I, D_I = 8, 128, 256, 8, 64
INDEX_TOPK = 256
H_B, DK_B, DV_B, RET_CHUNK, RET_THETA = 4, 128, 256, 128, 10000.0
H_QK, H_V, DK_C, DV_C, CONV_K, DN_CHUNK = 16, 32, 128, 128, 4, 64

MIX_A = H_A * DH_A
MIX_B = H_B * DV_B
MIX_AB = MIX_A + MIX_B
MIX_C = H_V * DV_C

LANES = 128
INT_MIN = np.int32(-2**31)

AB_QA, AB_VB, AB_GATE, AB_QI, AB_QB, AB_KB, AB_C, AB_KW, AB_NP = 0, 1024, 2048, 4096, 4608, 5120, 5632, 5888, 6144
DN_Q, DN_K, DN_V, DN_Z, DN_BA, DN_NP = 0, 2048, 4096, 8192, 12288, 12800

VMEM_LIMIT = 56 * 1024 * 1024


def _sigmoid(x):
    return 1.0 / (1.0 + jnp.exp(-x))


def _silu(x):
    return x * (0.5 * jnp.tanh(0.5 * x) + 0.5)


def _norm_proj_kernel(x_ref, g_ref, w_ref, o_ref, xn_ref):
    @pl.when(pl.program_id(1) == 0)
    def _():
        x = x_ref[...]
        ms = jnp.mean(x * x, axis=-1, keepdims=True)
        xn_ref[...] = (x * lax.rsqrt(ms + EPS) * g_ref[...]).astype(BF16)

    o_ref[...] = jnp.dot(xn_ref[...], w_ref[...], preferred_element_type=F32)


def _norm_proj(x2d, gain, w, tm, tn):
    T, D = x2d.shape
    N = w.shape[1]
    return pl.pallas_call(
        _norm_proj_kernel,
        grid=(T // tm, N // tn),
        in_specs=[
            pl.BlockSpec((tm, D), lambda i, j: (i, 0)),
            pl.BlockSpec((1, D), lambda i, j: (0, 0)),
            pl.BlockSpec((D, tn), lambda i, j: (0, j)),
        ],
        out_specs=pl.BlockSpec((tm, tn), lambda i, j: (i, j)),
        out_shape=jax.ShapeDtypeStruct((T, N), F32),
        scratch_shapes=[pltpu.VMEM((tm, D), BF16)],
        compiler_params=pltpu.CompilerParams(
            dimension_semantics=("parallel", "arbitrary"), vmem_limit_bytes=VMEM_LIMIT),
        name="norm_proj",
    )(x2d, gain, w)


def _dsa_kernel(qa_ref, c_ref, qi_ref, kwq_ref, kwk_ref, gate_ref, kvn_ref, wuk_ref, kn_ref, qn_ref, wuv_ref,
                o_ref, cnT_s, k_s, ki_s, key_s, bias_s, lg_s, j_s, qall_s, lat_s, *, S, Q, topk):
    qb = pl.program_id(1)
    nkb = qb + 1
    RB = 128
    nrb = nkb * (Q // RB)
    NT = (((1,), (1,)), ((), ()))

    @pl.when(qb == 0)
    def _prep():
        def body(j, _):
            r0 = pl.multiple_of(j * Q, Q)
            c = c_ref[pl.ds(r0, Q), :]
            cn = c * lax.rsqrt(jnp.mean(c * c, axis=-1, keepdims=True) + EPS) * kvn_ref[...]
            kk = jnp.dot(cn.astype(BF16), wuk_ref[...], preferred_element_type=F32)
            kk = kk * lax.rsqrt(jnp.mean(kk * kk, axis=-1, keepdims=True) + EPS) * kn_ref[...]
            cnT_s[j] = cn.T.astype(BF16)
            k_s[pl.ds(r0, Q), :] = kk.astype(BF16)
            ki_s[pl.ds(r0, Q), :] = kwk_ref[pl.ds(r0, Q), 0:D_I].astype(BF16)
            return 0
        lax.fori_loop(0, S // Q, body, 0)

    tpos = qb * Q + lax.broadcasted_iota(I32, (1, Q), 1)
    w_t = kwq_ref[...].T[D_I:D_I + H_I, :] * (H_I ** -0.5 * D_I ** -0.5)

    def score_blk(j, _):
        r0 = pl.multiple_of(j * Q, Q)
        kib = ki_s[pl.ds(r0, Q), :]
        acc = jnp.zeros((Q, Q), F32)
        for h in range(H_I):
            qih = qi_ref[:, h * D_I:(h + 1) * D_I].astype(BF16)
            s = lax.dot_general(kib, qih, NT, preferred_element_type=F32)
            acc = acc + jnp.maximum(s, 0.0) * w_t[h:h + 1, :]
        acc = acc + 0.0
        bits = pltpu.bitcast(acc, I32)
        key = bits ^ ((bits >> 31) & np.int32(0x7FFFFFFF))
        kpos = r0 + lax.broadcasted_iota(I32, (Q, Q), 0)
        key_s[pl.ds(r0, Q), :] = jnp.where(kpos <= tpos, key, INT_MIN)
        return 0
    lax.fori_loop(0, nkb, score_blk, 0)

    def count(pred):
        def body(i, acc):
            r0 = pl.multiple_of(i * RB, RB)
            kpos = r0 + lax.broadcasted_iota(I32, (RB, Q), 0)
            return acc + jnp.where(pred(key_s[pl.ds(r0, RB), :], kpos), 1.0, 0.0)
        acc = lax.fori_loop(0, nrb, body, jnp.zeros((RB, Q), F32))
        return jnp.sum(acc, axis=0, keepdims=True)

    def bit_body(i, prefix_u):
        cand_u = prefix_u | jnp.left_shift(np.int32(1), 31 - i)
        cand_s = cand_u ^ INT_MIN
        cnt = count(lambda k, _: k >= cand_s)
        return jnp.where(cnt >= float(topk), cand_u, prefix_u)
    thr = lax.fori_loop(0, 32, bit_body, jnp.zeros((1, Q), I32)) ^ INT_MIN

    cnt_ge = count(lambda k, _: k >= thr)
    ambiguous = (cnt_ge != float(topk)) & (tpos >= topk - 1)
    j_s[...] = jnp.full((1, Q), S, I32)

    @pl.when(jnp.max(jnp.where(ambiguous, 1.0, 0.0)) > 0.0)
    def _ties():
        need = float(topk) - count(lambda k, _: k > thr)
        nbits = int(S).bit_length() - 1

        def jbit(i, prefix):
            cand = prefix | jnp.left_shift(np.int32(1), nbits - 1 - i)
            cnt = count(lambda k, kpos: (k == thr) & (kpos < cand))
            return jnp.where(cnt < need, cand, prefix)
        jstar = lax.fori_loop(0, nbits, jbit, jnp.zeros((1, Q), I32))
        j_s[...] = jnp.where(ambiguous, jstar, S)

    jsel = j_s[...]

    def bias_blk(i, _):
        r0 = pl.multiple_of(i * RB, RB)
        k = key_s[pl.ds(r0, RB), :]
        kpos = r0 + lax.broadcasted_iota(I32, (RB, Q), 0)
        sel = (kpos <= tpos) & ((k > thr) | ((k == thr) & (kpos <= jsel)))
        bias_s[pl.ds(r0, RB), :] = jnp.where(sel, 0.0, -jnp.inf)
        return 0
    lax.fori_loop(0, nrb, bias_blk, 0)

    for h in range(H_A):
        qh = qa_ref[:, h * DH_A:(h + 1) * DH_A]
        qh = qh * lax.rsqrt(jnp.mean(qh * qh, axis=-1, keepdims=True) + EPS) * qn_ref[...] * (DH_A ** -0.5)
        qall_s[h * Q:(h + 1) * Q, :] = qh.astype(BF16)

    def pass_a(j, m):
        r0 = pl.multiple_of(j * Q, Q)
        s = lax.dot_general(k_s[pl.ds(r0, Q), :], qall_s[...], NT, preferred_element_type=F32)
        bias = bias_s[pl.ds(r0, Q), :]
        ms = []
        for h in range(H_A):
            sh = s[:, h * Q:(h + 1) * Q] + bias
            lg_s[pl.ds(r0, Q), h * Q:(h + 1) * Q] = sh
            ms.append(jnp.max(sh, axis=0, keepdims=True))
        return jnp.maximum(m, jnp.concatenate(ms, axis=1))
    m = lax.fori_loop(0, nkb, pass_a, jnp.full((1, H_A * Q), -jnp.inf, F32))

    lat_s[...] = jnp.zeros_like(lat_s)

    def pass_b(j, l):
        r0 = pl.multiple_of(j * Q, Q)
        p = jnp.exp(lg_s[pl.ds(r0, Q), :] - m)
        lat_s[...] += jnp.dot(cnT_s[j], p.astype(BF16), preferred_element_type=F32)
        return l + jnp.sum(p, axis=0, keepdims=True)
    l = lax.fori_loop(0, nkb, pass_b, jnp.zeros((1, H_A * Q), F32))

    lat = (lat_s[...] * (1.0 / l)).astype(BF16)
    for h in range(H_A):
        cols = slice(h * DH_A, (h + 1) * DH_A)
        out_h = lax.dot_general(lat[:, h * Q:(h + 1) * Q], wuv_ref[h], (((0,), (0,)), ((), ())),
                                preferred_element_type=F32)
        o_ref[:, cols] = (out_h * _silu(gate_ref[:, cols])).astype(o_ref.dtype)


def _dsa(proj, kv_norm, w_uk, k_norm, q_norm, w_uv, Q=256):
    B, S, _ = proj.shape
    topk = min(INDEX_TOPK, S // 4)
    assert S % Q == 0 and topk <= Q and (S & (S - 1)) == 0
    kern = functools.partial(_dsa_kernel, S=S, Q=Q, topk=topk)
    full = lambda *shape: pl.BlockSpec(shape, lambda b, q: (0,) * len(shape))
    return pl.pallas_call(
        kern,
        grid=(B, S // Q),
        in_specs=[
            pl.BlockSpec((None, Q, MIX_A), lambda b, q: (b, q, AB_QA // MIX_A)),
            pl.BlockSpec((None, S, D_C), lambda b, q: (b, 0, AB_C // D_C)),
            pl.BlockSpec((None, Q, H_I * D_I), lambda b, q: (b, q, AB_QI // (H_I * D_I))),
            pl.BlockSpec((None, Q, LANES), lambda b, q: (b, q, AB_KW // LANES)),
            pl.BlockSpec((None, S, LANES), lambda b, q: (b, 0, AB_KW // LANES)),
            pl.BlockSpec((None, Q, MIX_A), lambda b, q: (b, q, AB_GATE // MIX_A)),
            full(1, D_C), full(D_C, DH_A), full(1, DH_A), full(1, DH_A), full(H_A, D_C, DH_A),
        ],
        out_specs=pl.BlockSpec((None, Q, MIX_A), lambda b, q: (b, q, 0)),
        out_shape=jax.ShapeDtypeStruct((B, S, MIX_A), BF16),
        scratch_shapes=[
            pltpu.VMEM((S // Q, D_C, Q), BF16),
            pltpu.VMEM((S, DH_A), BF16),
            pltpu.VMEM((S, D_I), BF16),
            pltpu.VMEM((S, Q), I32),
            pltpu.VMEM((S, Q), F32),
            pltpu.VMEM((S, H_A * Q), F32),
            pltpu.VMEM((1, Q), I32),
            pltpu.VMEM((H_A * Q, DH_A), BF16),
            pltpu.VMEM((D_C, H_A * Q), F32),
        ],
        compiler_params=pltpu.CompilerParams(
            dimension_semantics=("parallel", "arbitrary"), vmem_limit_bytes=VMEM_LIMIT),
        name="dsa_attention",
    )(proj, proj, proj, proj, proj, proj, kv_norm, w_uk, k_norm, q_norm, w_uv)


def _ret_kernel(lg_ref, q_ref, k_ref, v_ref, gate_ref, cos_ref, sin_ref, g_ref, o_ref, *, S, C):
    h = pl.program_id(1)
    lg = lg_ref[h]
    NT = (((1,), (1,)), ((), ()))
    TN = (((0,), (0,)), ((), ()))
    N = S // C
    ii = lax.broadcasted_iota(I32, (C, C), 0)
    jj = lax.broadcasted_iota(I32, (C, C), 1)
    diff = (ii - jj).astype(F32)
    dmask = jnp.where(diff >= 0, jnp.exp(jnp.maximum(diff, 0.0) * lg), 0.0)
    icol = lax.broadcasted_iota(I32, (C, 1), 0).astype(F32)
    k_dec = jnp.exp((C - 1 - icol) * lg)
    q_dec = jnp.exp((icol + 1.0) * lg)
    chunk_decay = jnp.exp(jnp.full((1, DV_B), float(C), F32) * lg)

    def rot(x, cos, sin):
        return x * cos + pltpu.roll(x, DK_B // 2, axis=1) * sin

    q, k, v = [], [], []
    for n in range(N):
        rows = slice(n * C, (n + 1) * C)
        cos, sin = cos_ref[rows, :], sin_ref[rows, :]
        q.append(rot(q_ref[rows, :], cos, sin))
        k.append(rot(k_ref[rows, :], cos, sin) * (DK_B ** -0.5))
        v.append(v_ref[rows, :])
    inner = [lax.dot_general(q[n], k[n], NT, preferred_element_type=F32) * dmask for n in range(N)]
    kv = [lax.dot_general(k[n] * k_dec, v[n], TN, preferred_element_type=F32) for n in range(N - 1)]
    out = [jnp.dot(inner[n], v[n], preferred_element_type=F32) for n in range(N)]
    st = kv[0]
    for n in range(1, N):
        out[n] = out[n] + jnp.dot(q[n] * q_dec, st, preferred_element_type=F32)
        if n + 1 < N:
            st = st * chunk_decay + kv[n]
    for n in range(N):
        rows = slice(n * C, (n + 1) * C)
        o = out[n] * lax.rsqrt(jnp.mean(out[n] * out[n], axis=-1, keepdims=True) + EPS) * g_ref[...]
        o_ref[rows, :] = (o * _silu(gate_ref[rows, :])).astype(o_ref.dtype)


def _retention(proj, ret_norm):
    B, S, _ = proj.shape
    C = RET_CHUNK
    pos = jnp.arange(S, dtype=F32)
    inv_freq = 1.0 / (RET_THETA ** jnp.linspace(0.0, 1.0, DK_B // 2, dtype=F32))
    ang = pos[:, None] * inv_freq[None, :]
    cos2 = jnp.concatenate([jnp.cos(ang), jnp.cos(ang)], axis=-1)
    sin2 = jnp.concatenate([-jnp.sin(ang), jnp.sin(ang)], axis=-1)
    log_gamma = jnp.log1p(-jnp.exp2(-5.0 - jnp.arange(H_B, dtype=F32)))
    kern = functools.partial(_ret_kernel, S=S, C=C)
    return pl.pallas_call(
        kern,
        grid_spec=pltpu.PrefetchScalarGridSpec(
            num_scalar_prefetch=1,
            grid=(B, H_B),
            in_specs=[
                pl.BlockSpec((None, S, DK_B), lambda b, h, lg: (b, 0, AB_QB // DK_B + h)),
                pl.BlockSpec((None, S, DK_B), lambda b, h, lg: (b, 0, AB_KB // DK_B + h)),
                pl.BlockSpec((None, S, DV_B), lambda b, h, lg: (b, 0, AB_VB // DV_B + h)),
                pl.BlockSpec((None, S, DV_B), lambda b, h, lg: (b, 0, (AB_GATE + MIX_A) // DV_B + h)),
                pl.BlockSpec((S, DK_B), lambda b, h, lg: (0, 0)),
                pl.BlockSpec((S, DK_B), lambda b, h, lg: (0, 0)),
                pl.BlockSpec((None, 1, DV_B), lambda b, h, lg: (h, 0, 0)),
            ],
            out_specs=pl.BlockSpec((None, S, DV_B), lambda b, h, lg: (b, 0, h)),
        ),
        out_shape=jax.ShapeDtypeStruct((B, S, MIX_B), BF16),
        compiler_params=pltpu.CompilerParams(
            dimension_semantics=("parallel", "arbitrary"), vmem_limit_bytes=VMEM_LIMIT),
        name="retention",
    )(log_gamma, proj, proj, proj, proj, cos2, sin2, ret_norm.reshape(H_B, 1, DV_B))


def _ab_out_kernel(a_ref, b_ref, x_ref, w_ref, o_ref):
    o_ref[...] = (x_ref[...] + jnp.dot(a_ref[...], w_ref[:MIX_A, :], preferred_element_type=F32)
                  + jnp.dot(b_ref[...], w_ref[MIX_A:, :], preferred_element_type=F32))


def _ab_out(a2d, b2d, x2d, w, tm=512, tn=1024):
    T, D = x2d.shape
    return pl.pallas_call(
        _ab_out_kernel,
        grid=(T // tm, D // tn),
        in_specs=[
            pl.BlockSpec((tm, MIX_A), lambda i, j: (i, 0)),
            pl.BlockSpec((tm, MIX_B), lambda i, j: (i, 0)),
            pl.BlockSpec((tm, tn), lambda i, j: (i, j)),
            pl.BlockSpec((MIX_AB, tn), lambda i, j: (0, j)),
        ],
        out_specs=pl.BlockSpec((tm, tn), lambda i, j: (i, j)),
        out_shape=jax.ShapeDtypeStruct((T, D), F32),
        compiler_params=pltpu.CompilerParams(
            dimension_semantics=("parallel", "arbitrary"), vmem_limit_bytes=VMEM_LIMIT),
        name="ab_out_proj",
    )(a2d, b2d, x2d, w)


def _gdn_gates_kernel(ba_ref, alog_ref, dt_ref, beta_ref, gc_ref, *, S, C):
    ba = ba_ref[...]
    beta_ref[...] = _sigmoid(ba)
    xg = ba + dt_ref[...]
    softplus = jnp.maximum(xg, 0.0) + jnp.log(1.0 + jnp.exp(-jnp.abs(xg)))
    g_all = -jnp.exp(alog_ref[...]) * softplus
    rc = lax.broadcasted_iota(I32, (S, LANES), 0) & (C - 1)
    sh = 1
    while sh < C:
        g_all = g_all + jnp.where(rc >= sh, pltpu.roll(g_all, sh, axis=0), 0.0)
        sh *= 2
    gc_ref[...] = g_all


def _gdn_gates(proj, a_log, dt_bias):
    B, S, _ = proj.shape
    alog_row = jnp.zeros((1, LANES), F32).at[0, H_V:2 * H_V].set(a_log)
    dt_row = jnp.zeros((1, LANES), F32).at[0, H_V:2 * H_V].set(dt_bias)
    row = pl.BlockSpec((1, LANES), lambda b: (0, 0))
    out = pl.BlockSpec((None, S, LANES), lambda b: (b, 0, 0))
    return pl.pallas_call(
        functools.partial(_gdn_gates_kernel, S=S, C=DN_CHUNK),
        grid=(B,),
        in_specs=[pl.BlockSpec((None, S, LANES), lambda b: (b, 0, DN_BA // LANES)), row, row],
        out_specs=[out, out],
        out_shape=[jax.ShapeDtypeStruct((B, S, LANES), F32)] * 2,
        compiler_params=pltpu.CompilerParams(dimension_semantics=("parallel",), vmem_limit_bytes=VMEM_LIMIT),
        name="gdn_gates",
    )(proj, alog_row, dt_row)


def _gdn_kernel(q_ref, k_ref, v_ref, z_ref, beta_ref, gc_ref, cwq_ref, cwk_ref, cwv_ref, on_ref,
                o_ref, qn_s, kn_s, va_s, bb_s, gb_s, lhs_s, au_s, ku_s, gl_s, *, S, C, GC):
    g = pl.program_id(1)
    NG = S // (C * GC)
    REP = H_V // H_QK
    NT = (((1,), (1,)), ((), ()))
    TN = (((0,), (0,)), ((), ()))

    GR = GC * C
    HALO = 8

    def prep(gi):
        r0 = gi * GR

        def conv_silu(ref, w):
            if gi == 0:
                xw = jnp.concatenate([jnp.zeros((HALO, ref.shape[1]), F32), ref[0:GR, :]], axis=0)
            else:
                xw = ref[r0 - HALO:r0 + GR, :]
            y = xw[HALO:] * w[CONV_K - 1:CONV_K, :]
            for d in range(1, CONV_K):
                y = y + pltpu.roll(xw, d, axis=0)[HALO:] * w[CONV_K - 1 - d:CONV_K - d, :]
            return y * (0.5 * jnp.tanh(0.5 * y) + 0.5)

        def l2n(x):
            return x * lax.rsqrt(jnp.sum(x * x, axis=-1, keepdims=True) + EPS)

        qn_s[r0:r0 + GR, :] = l2n(conv_silu(q_ref, cwq_ref[...])) * (DK_C ** -0.5)
        kn_s[r0:r0 + GR, :] = l2n(conv_silu(k_ref, cwk_ref[...]))
        va_s[r0:r0 + GR, :] = conv_silu(v_ref, cwv_ref[...])
        lane = lax.broadcasted_iota(I32, (GR, LANES), 1)
        for j in range(REP):
            hv = g * REP + j
            bcol = jnp.sum(jnp.where(lane == hv, beta_ref[r0:r0 + GR, :], 0.0), axis=-1, keepdims=True)
            gcol = jnp.sum(jnp.where(lane == hv + H_V, gc_ref[r0:r0 + GR, :], 0.0), axis=-1, keepdims=True)
            bb_s[j, r0:r0 + GR, :] = jnp.broadcast_to(bcol, (GR, LANES))
            gb_s[j, r0:r0 + GR, :] = jnp.broadcast_to(gcol, (GR, LANES))

    assert REP * C == LANES
    ii = lax.broadcasted_iota(I32, (REP * C, REP * C), 0)
    jj = lax.broadcasted_iota(I32, (REP * C, REP * C), 1)
    same_head = (ii >= C) == (jj >= C)
    tri = same_head & (ii >= jj)
    strict = same_head & (ii > jj)
    eye = jnp.where(ii == jj, 1.0, 0.0)

    dot = functools.partial(jnp.dot, preferred_element_type=F32)
    chains = [(c, j) for c in range(GC) for j in range(REP)]

    def row0(gi, c):
        r = (gi * GC + c) * C
        return r if isinstance(r, int) else pl.multiple_of(r, C)

    def scan_step(gi, c, states):
        r0 = row0(gi, c)
        for j in range(REP):
            i = c * REP + j
            r = dot(lhs_s[i], states[j].astype(BF16))
            o = r[DK_C:] + au_s[i]
            states[j] = states[j] * gl_s[i, 0:1, :] + ku_s[i] - r[:DK_C]
            o = o * lax.rsqrt(jnp.mean(o * o, axis=-1, keepdims=True) + EPS) * on_ref[...]
            z = z_ref[pl.ds(r0, C), j * DV_C:(j + 1) * DV_C]
            o = o * (z * (0.5 * jnp.tanh(0.5 * z) + 0.5))
            o_ref[pl.ds(r0, C), j * DV_C:(j + 1) * DV_C] = o.astype(o_ref.dtype)

    def stage(gi, pending):
        pending = list(pending)

        def tick():
            if pending:
                pending.pop(0)()
        r0s = [row0(gi, c) for c in range(GC)]
        qc = [qn_s[pl.ds(r0s[c], C), :] for c in range(GC)]
        kc = [kn_s[pl.ds(r0s[c], C), :] for c in range(GC)]
        qkk = [lax.dot_general(jnp.concatenate([qc[c], kc[c]], axis=0).astype(BF16),
                               jnp.concatenate([kc[c], kc[c]], axis=0).astype(BF16),
                               NT, preferred_element_type=F32) for c in range(GC)]
        tick()
        gcb, bb, p, attn = {}, {}, [], []
        for c in range(GC):
            for j in range(REP):
                gcb[c, j] = gb_s[j, pl.ds(r0s[c], C), :]
                bb[c, j] = bb_s[j, pl.ds(r0s[c], C), :]
            g2 = jnp.concatenate([gcb[c, j] for j in range(REP)], axis=0)
            b2 = jnp.concatenate([bb[c, j] for j in range(REP)], axis=0)
            gcr = jnp.sum(jnp.where(ii == jj, g2, 0.0), axis=0, keepdims=True)
            decay = jnp.exp(jnp.where(tri, g2 - gcr, -jnp.inf))
            kk2 = jnp.concatenate([qkk[c][C:]] * REP, axis=0)
            qk2 = jnp.concatenate([qkk[c][:C]] * REP, axis=0)
            p.append(jnp.where(strict, -(kk2 * b2 * decay), 0.0))
            attn.append(qk2 * decay)
        qm = [dot(p[c].astype(BF16), p[c].astype(BF16)) for c in range(GC)]
        tick()
        sm = [eye + p[c] for c in range(GC)]
        for _ in range(4):
            r = [dot(qm[c].astype(BF16), jnp.concatenate([qm[c], sm[c]], axis=1).astype(BF16)) for c in range(GC)]
            tick()
            qm = [r[c][:, :REP * C] for c in range(GC)]
            sm = [sm[c] + r[c][:, REP * C:] for c in range(GC)]
        tinv = [sm[c] + dot(qm[c].astype(BF16), sm[c].astype(BF16)) for c in range(GC)]
        tick()
        uw2, glast = [], {}
        for c in range(GC):
            rhs = []
            for j in range(REP):
                vb = va_s[pl.ds(r0s[c], C), j * DV_C:(j + 1) * DV_C] * bb[c, j]
                kbg = kc[c] * bb[c, j] * jnp.exp(gcb[c, j])
                rhs.append(jnp.concatenate([vb, kbg], axis=1))
                glast[c, j] = gcb[c, j][C - 1:C, :]
            uw2.append(dot(tinv[c].astype(BF16), jnp.concatenate(rhs, axis=0).astype(BF16)))
        tick()
        uw = {(c, j): uw2[c][j * C:(j + 1) * C] for c, j in chains}
        ktuw = {(c, j): lax.dot_general(kc[c] * jnp.exp(glast[c, j] - gcb[c, j]), uw[c, j], TN,
                                        preferred_element_type=F32) for c, j in chains}
        tick()
        atuw2 = [dot(attn[c].astype(BF16), uw2[c].astype(BF16)) for c in range(GC)]
        atuw = {(c, j): atuw2[c][j * C:(j + 1) * C] for c, j in chains}
        while pending:
            tick()
        for c, j in chains:
            i = c * REP + j
            lhs_s[i] = jnp.concatenate([ktuw[c, j][:, DV_C:], qc[c] * jnp.exp(gcb[c, j]) - atuw[c, j][:, DV_C:]],
                                       axis=0).astype(BF16)
            au_s[i] = atuw[c, j][:, :DV_C]
            ku_s[i] = ktuw[c, j][:, :DV_C]
            gl_s[i] = jnp.broadcast_to(jnp.exp(glast[c, j]), (8, LANES))

    states = [jnp.zeros((DK_C, DV_C), F32) for _ in range(REP)]
    prep(0)
    for gi in range(NG):
        if gi + 1 < NG:
            prep(gi + 1)
        stage(gi, [functools.partial(scan_step, gi - 1, c, states) for c in range(GC)] if gi else [])
    for c in range(GC):
        scan_step(NG - 1, c, states)


def _gdn(proj, conv_w, a_log, dt_bias, out_norm):
    B, S, _ = proj.shape
    C = DN_CHUNK
    REP = H_V // H_QK
    VW = REP * DV_C
    GC = 8
    assert S % (C * GC) == 0
    beta_all, gc_all = _gdn_gates(proj, a_log, dt_bias)
    kern = functools.partial(_gdn_kernel, S=S, C=C, GC=GC)
    return pl.pallas_call(
        kern,
        grid=(B, H_QK),
        in_specs=[
            pl.BlockSpec((None, S, DK_C), lambda b, g: (b, 0, DN_Q // DK_C + g)),
            pl.BlockSpec((None, S, DK_C), lambda b, g: (b, 0, DN_K // DK_C + g)),
            pl.BlockSpec((None, S, VW), lambda b, g: (b, 0, DN_V // VW + g)),
            pl.BlockSpec((None, S, VW), lambda b, g: (b, 0, DN_Z // VW + g)),
            pl.BlockSpec((None, S, LANES), lambda b, g: (b, 0, 0)),
            pl.BlockSpec((None, S, LANES), lambda b, g: (b, 0, 0)),
            pl.BlockSpec((CONV_K, DK_C), lambda b, g: (0, g)),
            pl.BlockSpec((CONV_K, DK_C), lambda b, g: (0, H_QK + g)),
            pl.BlockSpec((CONV_K, VW), lambda b, g: (0, H_QK + g)),
            pl.BlockSpec((1, DV_C), lambda b, g: (0, 0)),
        ],
        out_specs=pl.BlockSpec((None, S, VW), lambda b, g: (b, 0, g)),
        out_shape=jax.ShapeDtypeStruct((B, S, MIX_C), BF16),
        scratch_shapes=[
            pltpu.VMEM((S, DK_C), F32),
            pltpu.VMEM((S, DK_C), F32),
            pltpu.VMEM((S, VW), F32),
            pltpu.VMEM((REP, S, LANES), F32),
            pltpu.VMEM((REP, S, LANES), F32),
            pltpu.VMEM((GC * REP, DK_C + C, DK_C), BF16),
            pltpu.VMEM((GC * REP, C, DV_C), F32),
            pltpu.VMEM((GC * REP, DK_C, DV_C), F32),
            pltpu.VMEM((GC * REP, 8, LANES), F32),
        ],
        compiler_params=pltpu.CompilerParams(
            dimension_semantics=("parallel", "arbitrary"), vmem_limit_bytes=VMEM_LIMIT),
        name="gated_deltanet",
    )(proj, proj, proj, proj, beta_all, gc_all, conv_w, conv_w, conv_w, out_norm.reshape(1, DV_C))


def _dn_out_kernel(o_in_ref, x_ref, w_ref, o_ref):
    o_ref[...] = x_ref[...] + jnp.dot(o_in_ref[...], w_ref[...], preferred_element_type=F32)


def _dn_out(o2d, x2d, w, tm=512, tn=1024):
    T, D = x2d.shape
    K = o2d.shape[1]
    return pl.pallas_call(
        _dn_out_kernel,
        grid=(T // tm, D // tn),
        in_specs=[
            pl.BlockSpec((tm, K), lambda i, j: (i, 0)),
            pl.BlockSpec((tm, tn), lambda i, j: (i, j)),
            pl.BlockSpec((K, tn), lambda i, j: (0, j)),
        ],
        out_specs=pl.BlockSpec((tm, tn), lambda i, j: (i, j)),
        out_shape=jax.ShapeDtypeStruct((T, D), F32),
        compiler_params=pltpu.CompilerParams(
            dimension_semantics=("parallel", "arbitrary"), vmem_limit_bytes=VMEM_LIMIT),
        name="dn_out_proj",
    )(o2d, x2d, w)


def _ab_weight(w):
    D = w.shape[0]
    o = np.cumsum([0, MIX_A, D_C, H_I * D_I, D_I, H_I, H_B * DK_B, H_B * DK_B, MIX_B, MIX_AB])
    q_a, c, q_i, k_i, w_i, q_b, k_b, v_b, gate = [w[:, o[i]:o[i + 1]] for i in range(9)]
    pad_kw = jnp.zeros((D, LANES - D_I - H_I), w.dtype)
    pad_end = jnp.zeros((D, AB_NP - AB_KW - LANES), w.dtype)
    return jnp.concatenate([q_a, v_b, gate, q_i, q_b, k_b, c, k_i, w_i, pad_kw, pad_end], axis=1).astype(BF16)


def _dn_weight(w):
    D = w.shape[0]
    conv_ch = 2 * H_QK * DK_C + MIX_C
    main = w[:, :conv_ch + MIX_C]
    ba = w[:, conv_ch + MIX_C:]
    pad = jnp.zeros((D, DN_NP - DN_BA - 2 * H_V), w.dtype)
    return jnp.concatenate([main, ba, pad], axis=1).astype(BF16)


def kernel(x, ab_norm, ab_w_in, ab_kv_norm, ab_w_uk, ab_w_uv, ab_q_norm, ab_k_norm, ab_ret_norm, ab_w_out,
           dn_norm, dn_w_in, dn_conv, dn_a_log, dn_dt_bias, dn_out_norm, dn_w_out):
    B, S, D = x.shape
    T = B * S
    depth = ab_norm.shape[0] + dn_norm.shape[0]
    h = x.astype(F32).reshape(T, D)
    for layer in range(depth):
        j = layer // 2
        if layer % 2 == 0:
            proj = _norm_proj(h, ab_norm[j].reshape(1, D), _ab_weight(ab_w_in[j]), tm=1024, tn=1024)
            proj3 = proj.reshape(B, S, AB_NP)
            a_out = _dsa(proj3, ab_kv_norm[j].reshape(1, D_C), ab_w_uk[j].astype(BF16),
                         ab_k_norm[j].reshape(1, DH_A), ab_q_norm[j].reshape(1, DH_A), ab_w_uv[j].astype(BF16))
            b_out = _retention(proj3, ab_ret_norm[j])
            h = _ab_out(a_out.reshape(T, MIX_A), b_out.reshape(T, MIX_B), h, ab_w_out[j].astype(BF16))
        else:
            proj = _norm_proj(h, dn_norm[j].reshape(1, D), _dn_weight(dn_w_in[j]), tm=1024, tn=512)
            o = _gdn(proj.reshape(B, S, DN_NP), dn_conv[j], dn_a_log[j], dn_dt_bias[j], dn_out_norm[j])
            h = _dn_out(o.reshape(T, MIX_C), h, dn_w_out[j].astype(BF16))
    return h.reshape(B, S, D).astype(x.dtype)
```

```python
import functools

import numpy as np
import jax
import jax.numpy as jnp
from jax import lax
from jax.experimental import pallas as pl
from jax.experimental.pallas import tpu as pltpu

F32 = jnp.float32
BF16 = jnp.bfloat16
I32 = jnp.int32
I16 = jnp.int16

EPS = 1e-6
H_A, DH_A, D_C, H_I, D_I = 8, 128, 256, 8, 64
INDEX_TOPK = 256
H_B, DK_B, DV_B, RET_CHUNK, RET_THETA = 4, 128, 256, 128, 10000.0
H_QK, H_V, DK_C, DV_C, CONV_K, DN_CHUNK = 16, 32, 128, 128, 4, 64

MIX_A = H_A * DH_A
MIX_B = H_B * DV_B
MIX_AB = MIX_A + MIX_B
MIX_C = H_V * DV_C

LANES = 128
INT_MIN = np.int32(-2**31)

AB_QA, AB_VB, AB_GATE, AB_QI, AB_QB, AB_KB, AB_C, AB_KW, AB_NP = 0, 1024, 2048, 4096, 4608, 5120, 5632, 5888, 6144
DN_Q, DN_K, DN_V, DN_Z, DN_BA, DN_NP = 0, 2048, 4096, 8192, 12288, 12800

VMEM_LIMIT = 56 * 1024 * 1024


def _sigmoid(x):
    return 1.0 / (1.0 + jnp.exp(-x))


def _silu(x):
    return x * (0.5 * jnp.tanh(0.5 * x) + 0.5)


def _norm_proj_kernel(x_ref, g_ref, w_ref, o_ref, xn_ref):
    @pl.when(pl.program_id(1) == 0)
    def _():
        x = x_ref[...]
        ms = jnp.mean(x * x, axis=-1, keepdims=True)
        xn_ref[...] = (x * lax.rsqrt(ms + EPS) * g_ref[...]).astype(BF16)

    o_ref[...] = jnp.dot(xn_ref[...], w_ref[...], preferred_element_type=F32)


def _norm_proj(x2d, gain, w, tm, tn):
    T, D = x2d.shape
    N = w.shape[1]
    return pl.pallas_call(
        _norm_proj_kernel,
        grid=(T // tm, N // tn),
        in_specs=[
            pl.BlockSpec((tm, D), lambda i, j: (i, 0)),
            pl.BlockSpec((1, D), lambda i, j: (0, 0)),
            pl.BlockSpec((D, tn), lambda i, j: (0, j)),
        ],
        out_specs=pl.BlockSpec((tm, tn), lambda i, j: (i, j)),
        out_shape=jax.ShapeDtypeStruct((T, N), F32),
        scratch_shapes=[pltpu.VMEM((tm, D), BF16)],
        compiler_params=pltpu.CompilerParams(
            dimension_semantics=("parallel", "arbitrary"), vmem_limit_bytes=VMEM_LIMIT),
        name="norm_proj",
    )(x2d, gain, w)


def _dsa_kernel(qa_ref, c_ref, qi_ref, kwq_ref, kwk_ref, gate_ref, kvn_ref, wuk_ref, kn_ref, qn_ref, wuv_ref,
                o_ref, cnT_s, k_s, ki_s, key_s, khi_s, klo_s, bias_s, lg_s, j_s, qall_s, lat_s, *, S, Q, topk):
    qb = pl.program_id(1)
    nkb = qb + 1
    RB = 128
    nrb = nkb * (Q // RB)
    NT = (((1,), (1,)), ((), ()))

    @pl.when(qb == 0)
    def _prep():
        def body(j, _):
            r0 = pl.multiple_of(j * Q, Q)
            c = c_ref[pl.ds(r0, Q), :]
            cn = c * lax.rsqrt(jnp.mean(c * c, axis=-1, keepdims=True) + EPS) * kvn_ref[...]
            kk = jnp.dot(cn.astype(BF16), wuk_ref[...], preferred_element_type=F32)
            kk = kk * lax.rsqrt(jnp.mean(kk * kk, axis=-1, keepdims=True) + EPS) * kn_ref[...]
            cnT_s[j] = cn.T.astype(BF16)
            k_s[pl.ds(r0, Q), :] = kk.astype(BF16)
            ki_s[pl.ds(r0, Q), :] = kwk_ref[pl.ds(r0, Q), 0:D_I].astype(BF16)
            return 0
        lax.fori_loop(0, S // Q, body, 0)

    tpos = qb * Q + lax.broadcasted_iota(I32, (1, Q), 1)
    w_t = kwq_ref[...].T[D_I:D_I + H_I, :] * (H_I ** -0.5 * D_I ** -0.5)

    def score_blk(j, _):
        r0 = pl.multiple_of(j * Q, Q)
        kib = ki_s[pl.ds(r0, Q), :]
        acc = jnp.zeros((Q, Q), F32)
        for h in range(H_I):
            qih = qi_ref[:, h * D_I:(h + 1) * D_I].astype(BF16)
            s = lax.dot_general(kib, qih, NT, preferred_element_type=F32)
            acc = acc + jnp.maximum(s, 0.0) * w_t[h:h + 1, :]
        acc = acc + 0.0
        bits = pltpu.bitcast(acc, I32)
        key = bits ^ ((bits >> 31) & np.int32(0x7FFFFFFF))
        kpos = r0 + lax.broadcasted_iota(I32, (Q, Q), 0)
        key = jnp.where(kpos <= tpos, key, INT_MIN)
        key_s[pl.ds(r0, Q), :] = key
        khi_s[pl.ds(r0, Q), :] = (key >> 16).astype(I16)
        return 0
    lax.fori_loop(0, nkb, score_blk, 0)

    def count(pred):
        def body(i, acc):
            r0 = pl.multiple_of(i * RB, RB)
            kpos = r0 + lax.broadcasted_iota(I32, (RB, Q), 0)
            return acc + jnp.where(pred(key_s[pl.ds(r0, RB), :], kpos), 1.0, 0.0)
        acc = lax.fori_loop(0, nrb, body, jnp.zeros((RB, Q), F32))
        return jnp.sum(acc, axis=0, keepdims=True)

    def count16(ref, pred):
        def body(j, acc):
            r0 = pl.multiple_of(j * Q, Q)
            return acc + jnp.where(pred(ref[pl.ds(r0, Q), :]), np.int16(1), np.int16(0))
        acc = lax.fori_loop(0, nkb, body, jnp.zeros((Q, Q), I16))
        acc = jnp.sum(acc.reshape(Q // 16, 16, Q), axis=0)
        return jnp.sum(acc.astype(F32), axis=0, keepdims=True)

    def search16(ref, base):
        def bit_body(i, prefix_u):
            cand_u = prefix_u | jnp.left_shift(np.int32(1), 15 - i)
            cand_s = (cand_u - HALF16).astype(I16)
            cnt = base + count16(ref, lambda k: k >= cand_s)
            return jnp.where(cnt >= float(topk), cand_u, prefix_u)
        return lax.fori_loop(0, 16, bit_body, jnp.zeros((1, Q), I32))

    HALF16 = np.int32(1 << 15)
    th = search16(khi_s, 0.0) - HALF16
    th16 = th.astype(I16)
    cnt_hi_gt = count16(khi_s, lambda k: k > th16)

    def lo_blk(j, _):
        r0 = pl.multiple_of(j * Q, Q)
        k = key_s[pl.ds(r0, Q), :]
        lo = (k & np.int32(0xFFFF)) - HALF16
        klo_s[pl.ds(r0, Q), :] = jnp.where((k >> 16) == th, lo, -HALF16).astype(I16)
        return 0
    lax.fori_loop(0, nkb, lo_blk, 0)
    tl_u = search16(klo_s, cnt_hi_gt)
    thr = (th << 16) | tl_u

    tl16 = (tl_u - HALF16).astype(I16)
    cnt_ge = cnt_hi_gt + count16(klo_s, lambda k: k >= tl16)
    ambiguous = (cnt_ge != float(topk)) & (tpos >= topk - 1)
    j_s[...] = jnp.full((1, Q), S, I32)

    @pl.when(jnp.max(jnp.where(ambiguous, 1.0, 0.0)) > 0.0)
    def _ties():
        need = float(topk) - count(lambda k, _: k > thr)
        nbits = int(S).bit_length() - 1

        def jbit(i, prefix):
            cand = prefix | jnp.left_shift(np.int32(1), nbits - 1 - i)
            cnt = count(lambda k, kpos: (k == thr) & (kpos < cand))
            return jnp.where(cnt < need, cand, prefix)
        jstar = lax.fori_loop(0, nbits, jbit, jnp.zeros((1, Q), I32))
        j_s[...] = jnp.where(ambiguous, jstar, S)

    jsel = j_s[...]

    def bias_blk(i, _):
        r0 = pl.multiple_of(i * RB, RB)
        k = key_s[pl.ds(r0, RB), :]
        kpos = r0 + lax.broadcasted_iota(I32, (RB, Q), 0)
        sel = (kpos <= tpos) & ((k > thr) | ((k == thr) & (kpos <= jsel)))
        bias_s[pl.ds(r0, RB), :] = jnp.where(sel, 0.0, -jnp.inf)
        return 0
    lax.fori_loop(0, nrb, bias_blk, 0)

    for h in range(H_A):
        qh = qa_ref[:, h * DH_A:(h + 1) * DH_A]
        qh = qh * lax.rsqrt(jnp.mean(qh * qh, axis=-1, keepdims=True) + EPS) * qn_ref[...] * (DH_A ** -0.5)
        qall_s[h * Q:(h + 1) * Q, :] = qh.astype(BF16)

    def pass_a(j, m):
        r0 = pl.multiple_of(j * Q, Q)
        s = lax.dot_general(k_s[pl.ds(r0, Q), :], qall_s[...], NT, preferred_element_type=F32)
        bias = bias_s[pl.ds(r0, Q), :]
        ms = []
        for h in range(H_A):
            sh = s[:, h * Q:(h + 1) * Q] + bias
            lg_s[pl.ds(r0, Q), h * Q:(h + 1) * Q] = sh
            ms.append(jnp.max(sh, axis=0, keepdims=True))
        return jnp.maximum(m, jnp.concatenate(ms, axis=1))
    m = lax.fori_loop(0, nkb, pass_a, jnp.full((1, H_A * Q), -jnp.inf, F32))

    lat_s[...] = jnp.zeros_like(lat_s)

    def pass_b(j, l):
        r0 = pl.multiple_of(j * Q, Q)
        p = jnp.exp(lg_s[pl.ds(r0, Q), :] - m)
        lat_s[...] += jnp.dot(cnT_s[j], p.astype(BF16), preferred_element_type=F32)
        return l + jnp.sum(p, axis=0, keepdims=True)
    l = lax.fori_loop(0, nkb, pass_b, jnp.zeros((1, H_A * Q), F32))

    lat = (lat_s[...] * (1.0 / l)).astype(BF16)
    for h in range(H_A):
        cols = slice(h * DH_A, (h + 1) * DH_A)
        out_h = lax.dot_general(lat[:, h * Q:(h + 1) * Q], wuv_ref[h], (((0,), (0,)), ((), ())),
                                preferred_element_type=F32)
        o_ref[:, cols] = (out_h * _silu(gate_ref[:, cols])).astype(o_ref.dtype)


def _dsa(proj, kv_norm, w_uk, k_norm, q_norm, w_uv, Q=256):
    B, S, _ = proj.shape
    topk = min(INDEX_TOPK, S // 4)
    assert S % Q == 0 and topk <= Q and (S & (S - 1)) == 0
    kern = functools.partial(_dsa_kernel, S=S, Q=Q, topk=topk)
    full = lambda *shape: pl.BlockSpec(shape, lambda b, q: (0,) * len(shape))
    return pl.pallas_call(
        kern,
        grid=(B, S // Q),
        in_specs=[
            pl.BlockSpec((None, Q, MIX_A), lambda b, q: (b, q, AB_QA // MIX_A)),
            pl.BlockSpec((None, S, D_C), lambda b, q: (b, 0, AB_C // D_C)),
            pl.BlockSpec((None, Q, H_I * D_I), lambda b, q: (b, q, AB_QI // (H_I * D_I))),
            pl.BlockSpec((None, Q, LANES), lambda b, q: (b, q, AB_KW // LANES)),
            pl.BlockSpec((None, S, LANES), lambda b, q: (b, 0, AB_KW // LANES)),
            pl.BlockSpec((None, Q, MIX_A), lambda b, q: (b, q, AB_GATE // MIX_A)),
            full(1, D_C), full(D_C, DH_A), full(1, DH_A), full(1, DH_A), full(H_A, D_C, DH_A),
        ],
        out_specs=pl.BlockSpec((None, Q, MIX_A), lambda b, q: (b, q, 0)),
        out_shape=jax.ShapeDtypeStruct((B, S, MIX_A), BF16),
        scratch_shapes=[
            pltpu.VMEM((S // Q, D_C, Q), BF16),
            pltpu.VMEM((S, DH_A), BF16),
            pltpu.VMEM((S, D_I), BF16),
            pltpu.VMEM((S, Q), I32),
            pltpu.VMEM((S, Q), I16),
            pltpu.VMEM((S, Q), I16),
            pltpu.VMEM((S, Q), F32),
            pltpu.VMEM((S, H_A * Q), F32),
            pltpu.VMEM((1, Q), I32),
            pltpu.VMEM((H_A * Q, DH_A), BF16),
            pltpu.VMEM((D_C, H_A * Q), F32),
        ],
        compiler_params=pltpu.CompilerParams(
            dimension_semantics=("parallel", "arbitrary"), vmem_limit_bytes=VMEM_LIMIT),
        name="dsa_attention",
    )(proj, proj, proj, proj, proj, proj, kv_norm, w_uk, k_norm, q_norm, w_uv)


def _ret_kernel(lg_ref, q_ref, k_ref, v_ref, gate_ref, cos_ref, sin_ref, g_ref, o_ref, *, S, C):
    h = pl.program_id(1)
    lg = lg_ref[h]
    NT = (((1,), (1,)), ((), ()))
    TN = (((0,), (0,)), ((), ()))
    N = S // C
    ii = lax.broadcasted_iota(I32, (C, C), 0)
    jj = lax.broadcasted_iota(I32, (C, C), 1)
    diff = (ii - jj).astype(F32)
    dmask = jnp.where(diff >= 0, jnp.exp(jnp.maximum(diff, 0.0) * lg), 0.0)
    icol = lax.broadcasted_iota(I32, (C, 1), 0).astype(F32)
    k_dec = jnp.exp((C - 1 - icol) * lg)
    q_dec = jnp.exp((icol + 1.0) * lg)
    chunk_decay = jnp.exp(jnp.full((1, DV_B), float(C), F32) * lg)

    def rot(x, cos, sin):
        return x * cos + pltpu.roll(x, DK_B // 2, axis=1) * sin

    q, k, v = [], [], []
    for n in range(N):
        rows = slice(n * C, (n + 1) * C)
        cos, sin = cos_ref[rows, :], sin_ref[rows, :]
        q.append(rot(q_ref[rows, :], cos, sin))
        k.append(rot(k_ref[rows, :], cos, sin) * (DK_B ** -0.5))
        v.append(v_ref[rows, :])
    inner = [lax.dot_general(q[n], k[n], NT, preferred_element_type=F32) * dmask for n in range(N)]
    kv = [lax.dot_general(k[n] * k_dec, v[n], TN, preferred_element_type=F32) for n in range(N - 1)]
    out = [jnp.dot(inner[n], v[n], preferred_element_type=F32) for n in range(N)]
    st = kv[0]
    for n in range(1, N):
        out[n] = out[n] + jnp.dot(q[n] * q_dec, st, preferred_element_type=F32)
        if n + 1 < N:
            st = st * chunk_decay + kv[n]
    for n in range(N):
        rows = slice(n * C, (n + 1) * C)
        o = out[n] * lax.rsqrt(jnp.mean(out[n] * out[n], axis=-1, keepdims=True) + EPS) * g_ref[...]
        o_ref[rows, :] = (o * _silu(gate_ref[rows, :])).astype(o_ref.dtype)


def _retention(proj, ret_norm):
    B, S, _ = proj.shape
    C = RET_CHUNK
    pos = jnp.arange(S, dtype=F32)
    inv_freq = 1.0 / (RET_THETA ** jnp.linspace(0.0, 1.0, DK_B // 2, dtype=F32))
    ang = pos[:, None] * inv_freq[None, :]
    cos2 = jnp.concatenate([jnp.cos(ang), jnp.cos(ang)], axis=-1)
    sin2 = jnp.concatenate([-jnp.sin(ang), jnp.sin(ang)], axis=-1)
    log_gamma = jnp.log1p(-jnp.exp2(-5.0 - jnp.arange(H_B, dtype=F32)))
    kern = functools.partial(_ret_kernel, S=S, C=C)
    return pl.pallas_call(
        kern,
        grid_spec=pltpu.PrefetchScalarGridSpec(
            num_scalar_prefetch=1,
            grid=(B, H_B),
            in_specs=[
                pl.BlockSpec((None, S, DK_B), lambda b, h, lg: (b, 0, AB_QB // DK_B + h)),
                pl.BlockSpec((None, S, DK_B), lambda b, h, lg: (b, 0, AB_KB // DK_B + h)),
                pl.BlockSpec((None, S, DV_B), lambda b, h, lg: (b, 0, AB_VB // DV_B + h)),
                pl.BlockSpec((None, S, DV_B), lambda b, h, lg: (b, 0, (AB_GATE + MIX_A) // DV_B + h)),
                pl.BlockSpec((S, DK_B), lambda b, h, lg: (0, 0)),
                pl.BlockSpec((S, DK_B), lambda b, h, lg: (0, 0)),
                pl.BlockSpec((None, 1, DV_B), lambda b, h, lg: (h, 0, 0)),
            ],
            out_specs=pl.BlockSpec((None, S, DV_B), lambda b, h, lg: (b, 0, h)),
        ),
        out_shape=jax.ShapeDtypeStruct((B, S, MIX_B), BF16),
        compiler_params=pltpu.CompilerParams(
            dimension_semantics=("parallel", "arbitrary"), vmem_limit_bytes=VMEM_LIMIT),
        name="retention",
    )(log_gamma, proj, proj, proj, proj, cos2, sin2, ret_norm.reshape(H_B, 1, DV_B))


def _ab_out_kernel(a_ref, b_ref, x_ref, w_ref, o_ref):
    o_ref[...] = (x_ref[...] + jnp.dot(a_ref[...], w_ref[:MIX_A, :], preferred_element_type=F32)
                  + jnp.dot(b_ref[...], w_ref[MIX_A:, :], preferred_element_type=F32))


def _ab_out(a2d, b2d, x2d, w, tm=1024, tn=1024):
    T, D = x2d.shape
    return pl.pallas_call(
        _ab_out_kernel,
        grid=(T // tm, D // tn),
        in_specs=[
            pl.BlockSpec((tm, MIX_A), lambda i, j: (i, 0)),
            pl.BlockSpec((tm, MIX_B), lambda i, j: (i, 0)),
            pl.BlockSpec((tm, tn), lambda i, j: (i, j)),
            pl.BlockSpec((MIX_AB, tn), lambda i, j: (0, j)),
        ],
        out_specs=pl.BlockSpec((tm, tn), lambda i, j: (i, j)),
        out_shape=jax.ShapeDtypeStruct((T, D), F32),
        compiler_params=pltpu.CompilerParams(
            dimension_semantics=("parallel", "arbitrary"), vmem_limit_bytes=VMEM_LIMIT),
        name="ab_out_proj",
    )(a2d, b2d, x2d, w)


def _gdn_gates_kernel(ba_ref, alog_ref, dt_ref, beta_ref, gc_ref, *, S, C):
    ba = ba_ref[...]
    beta_ref[...] = _sigmoid(ba)
    xg = ba + dt_ref[...]
    softplus = jnp.maximum(xg, 0.0) + jnp.log(1.0 + jnp.exp(-jnp.abs(xg)))
    g_all = -jnp.exp(alog_ref[...]) * softplus
    rc = lax.broadcasted_iota(I32, (S, LANES), 0) & (C - 1)
    sh = 1
    while sh < C:
        g_all = g_all + jnp.where(rc >= sh, pltpu.roll(g_all, sh, axis=0), 0.0)
        sh *= 2
    gc_ref[...] = g_all


def _gdn_gates(proj, a_log, dt_bias):
    B, S, _ = proj.shape
    alog_row = jnp.zeros((1, LANES), F32).at[0, H_V:2 * H_V].set(a_log)
    dt_row = jnp.zeros((1, LANES), F32).at[0, H_V:2 * H_V].set(dt_bias)
    row = pl.BlockSpec((1, LANES), lambda b: (0, 0))
    out = pl.BlockSpec((None, S, LANES), lambda b: (b, 0, 0))
    return pl.pallas_call(
        functools.partial(_gdn_gates_kernel, S=S, C=DN_CHUNK),
        grid=(B,),
        in_specs=[pl.BlockSpec((None, S, LANES), lambda b: (b, 0, DN_BA // LANES)), row, row],
        out_specs=[out, out],
        out_shape=[jax.ShapeDtypeStruct((B, S, LANES), F32)] * 2,
        compiler_params=pltpu.CompilerParams(dimension_semantics=("parallel",), vmem_limit_bytes=VMEM_LIMIT),
        name="gdn_gates",
    )(proj, alog_row, dt_row)


def _gdn_kernel(q_ref, k_ref, v_ref, z_ref, beta_ref, gc_ref, cwq_ref, cwk_ref, cwv_ref, on_ref,
                o_ref, qn_s, kn_s, va_s, bb_s, gb_s, lhs_s, au_s, ku_s, gl_s, *, S, C, GC):
    g = pl.program_id(1)
    NG = S // (C * GC)
    REP = H_V // H_QK
    NT = (((1,), (1,)), ((), ()))
    TN = (((0,), (0,)), ((), ()))

    GR = GC * C
    HALO = 8

    def prep(gi):
        r0 = gi * GR

        def conv_silu(ref, w):
            if gi == 0:
                xw = jnp.concatenate([jnp.zeros((HALO, ref.shape[1]), F32), ref[0:GR, :]], axis=0)
            else:
                xw = ref[r0 - HALO:r0 + GR, :]
            y = xw[HALO:] * w[CONV_K - 1:CONV_K, :]
            for d in range(1, CONV_K):
                y = y + pltpu.roll(xw, d, axis=0)[HALO:] * w[CONV_K - 1 - d:CONV_K - d, :]
            return y * (0.5 * jnp.tanh(0.5 * y) + 0.5)

        def l2n(x):
            return x * lax.rsqrt(jnp.sum(x * x, axis=-1, keepdims=True) + EPS)

        qn_s[r0:r0 + GR, :] = l2n(conv_silu(q_ref, cwq_ref[...])) * (DK_C ** -0.5)
        kn_s[r0:r0 + GR, :] = l2n(conv_silu(k_ref, cwk_ref[...]))
        va_s[r0:r0 + GR, :] = conv_silu(v_ref, cwv_ref[...])
        lane = lax.broadcasted_iota(I32, (GR, LANES), 1)
        for j in range(REP):
            hv = g * REP + j
            bcol = jnp.sum(jnp.where(lane == hv, beta_ref[r0:r0 + GR, :], 0.0), axis=-1, keepdims=True)
            gcol = jnp.sum(jnp.where(lane == hv + H_V, gc_ref[r0:r0 + GR, :], 0.0), axis=-1, keepdims=True)
            bb_s[j, r0:r0 + GR, :] = jnp.broadcast_to(bcol, (GR, LANES))
            gb_s[j, r0:r0 + GR, :] = jnp.broadcast_to(gcol, (GR, LANES))

    assert REP * C == LANES
    ii = lax.broadcasted_iota(I32, (REP * C, REP * C), 0)
    jj = lax.broadcasted_iota(I32, (REP * C, REP * C), 1)
    same_head = (ii >= C) == (jj >= C)
    tri = same_head & (ii >= jj)
    strict = same_head & (ii > jj)
    eye = jnp.where(ii == jj, 1.0, 0.0)

    dot = functools.partial(jnp.dot, preferred_element_type=F32)
    chains = [(c, j) for c in range(GC) for j in range(REP)]

    def row0(gi, c):
        r = (gi * GC + c) * C
        return r if isinstance(r, int) else pl.multiple_of(r, C)

    def scan_step(gi, c, states):
        r0 = row0(gi, c)
        for j in range(REP):
            i = c * REP + j
            r = dot(lhs_s[i], states[j].astype(BF16))
            o = r[DK_C:] + au_s[i]
            states[j] = states[j] * gl_s[i, 0:1, :] + ku_s[i] - r[:DK_C]
            o = o * lax.rsqrt(jnp.mean(o * o, axis=-1, keepdims=True) + EPS) * on_ref[...]
            z = z_ref[pl.ds(r0, C), j * DV_C:(j + 1) * DV_C]
            o = o * (z * (0.5 * jnp.tanh(0.5 * z) + 0.5))
            o_ref[pl.ds(r0, C), j * DV_C:(j + 1) * DV_C] = o.astype(o_ref.dtype)

    def stage(gi, pending):
        pending = list(pending)

        def tick():
            if pending:
                pending.pop(0)()
        r0s = [row0(gi, c) for c in range(GC)]
        qc = [qn_s[pl.ds(r0s[c], C), :] for c in range(GC)]
        kc = [kn_s[pl.ds(r0s[c], C), :] for c in range(GC)]
        qkk = [lax.dot_general(jnp.concatenate([qc[c], kc[c]], axis=0).astype(BF16),
                               jnp.concatenate([kc[c], kc[c]], axis=0).astype(BF16),
                               NT, preferred_element_type=F32) for c in range(GC)]
        tick()
        gcb, bb, p, attn = {}, {}, [], []
        for c in range(GC):
            for j in range(REP):
                gcb[c, j] = gb_s[j, pl.ds(r0s[c], C), :]
                bb[c, j] = bb_s[j, pl.ds(r0s[c], C), :]
            g2 = jnp.concatenate([gcb[c, j] for j in range(REP)], axis=0)
            b2 = jnp.concatenate([bb[c, j] for j in range(REP)], axis=0)
            gcr = jnp.sum(jnp.where(ii == jj, g2, 0.0), axis=0, keepdims=True)
            decay = jnp.exp(jnp.where(tri, g2 - gcr, -jnp.inf))
            kk2 = jnp.concatenate([qkk[c][C:]] * REP, axis=0)
            qk2 = jnp.concatenate([qkk[c][:C]] * REP, axis=0)
            p.append(jnp.where(strict, -(kk2 * b2 * decay), 0.0))
            attn.append(qk2 * decay)
        qm = [dot(p[c].astype(BF16), p[c].astype(BF16)) for c in range(GC)]
        tick()
        sm = [eye + p[c] for c in range(GC)]
        for _ in range(4):
            r = [dot(qm[c].astype(BF16), jnp.concatenate([qm[c], sm[c]], axis=1).astype(BF16)) for c in range(GC)]
            tick()
            qm = [r[c][:, :REP * C] for c in range(GC)]
            sm = [sm[c] + r[c][:, REP * C:] for c in range(GC)]
        tinv = [sm[c] + dot(qm[c].astype(BF16), sm[c].astype(BF16)) for c in range(GC)]
        tick()
        uw2, glast = [], {}
        for c in range(GC):
            rhs = []
            for j in range(REP):
                vb = va_s[pl.ds(r0s[c], C), j * DV_C:(j + 1) * DV_C] * bb[c, j]
                kbg = kc[c] * bb[c, j] * jnp.exp(gcb[c, j])
                rhs.append(jnp.concatenate([vb, kbg], axis=1))
                glast[c, j] = gcb[c, j][C - 1:C, :]
            uw2.append(dot(tinv[c].astype(BF16), jnp.concatenate(rhs, axis=0).astype(BF16)))
        tick()
        uw = {(c, j): uw2[c][j * C:(j + 1) * C] for c, j in chains}
        ktuw = {(c, j): lax.dot_general(kc[c] * jnp.exp(glast[c, j] - gcb[c, j]), uw[c, j], TN,
                                        preferred_element_type=F32) for c, j in chains}
        tick()
        atuw2 = [dot(attn[c].astype(BF16), uw2[c].astype(BF16)) for c in range(GC)]
        atuw = {(c, j): atuw2[c][j * C:(j + 1) * C] for c, j in chains}
        while pending:
            tick()
        for c, j in chains:
            i = c * REP + j
            lhs_s[i] = jnp.concatenate([ktuw[c, j][:, DV_C:], qc[c] * jnp.exp(gcb[c, j]) - atuw[c, j][:, DV_C:]],
                                       axis=0).astype(BF16)
            au_s[i] = atuw[c, j][:, :DV_C]
            ku_s[i] = ktuw[c, j][:, :DV_C]
            gl_s[i] = jnp.broadcast_to(jnp.exp(glast[c, j]), (8, LANES))

    states = [jnp.zeros((DK_C, DV_C), F32) for _ in range(REP)]
    prep(0)
    for gi in range(NG):
        if gi + 1 < NG:
            prep(gi + 1)
        stage(gi, [functools.partial(scan_step, gi - 1, c, states) for c in range(GC)] if gi else [])
    for c in range(GC):
        scan_step(NG - 1, c, states)


def _gdn(proj, conv_w, a_log, dt_bias, out_norm):
    B, S, _ = proj.shape
    C = DN_CHUNK
    REP = H_V // H_QK
    VW = REP * DV_C
    GC = 8
    assert S % (C * GC) == 0
    beta_all, gc_all = _gdn_gates(proj, a_log, dt_bias)
    kern = functools.partial(_gdn_kernel, S=S, C=C, GC=GC)
    return pl.pallas_call(
        kern,
        grid=(B, H_QK),
        in_specs=[
            pl.BlockSpec((None, S, DK_C), lambda b, g: (b, 0, DN_Q // DK_C + g)),
            pl.BlockSpec((None, S, DK_C), lambda b, g: (b, 0, DN_K // DK_C + g)),
            pl.BlockSpec((None, S, VW), lambda b, g: (b, 0, DN_V // VW + g)),
            pl.BlockSpec((None, S, VW), lambda b, g: (b, 0, DN_Z // VW + g)),
            pl.BlockSpec((None, S, LANES), lambda b, g: (b, 0, 0)),
            pl.BlockSpec((None, S, LANES), lambda b, g: (b, 0, 0)),
            pl.BlockSpec((CONV_K, DK_C), lambda b, g: (0, g)),
            pl.BlockSpec((CONV_K, DK_C), lambda b, g: (0, H_QK + g)),
            pl.BlockSpec((CONV_K, VW), lambda b, g: (0, H_QK + g)),
            pl.BlockSpec((1, DV_C), lambda b, g: (0, 0)),
        ],
        out_specs=pl.BlockSpec((None, S, VW), lambda b, g: (b, 0, g)),
        out_shape=jax.ShapeDtypeStruct((B, S, MIX_C), BF16),
        scratch_shapes=[
            pltpu.VMEM((S, DK_C), F32),
            pltpu.VMEM((S, DK_C), F32),
            pltpu.VMEM((S, VW), F32),
            pltpu.VMEM((REP, S, LANES), F32),
            pltpu.VMEM((REP, S, LANES), F32),
            pltpu.VMEM((GC * REP, DK_C + C, DK_C), BF16),
            pltpu.VMEM((GC * REP, C, DV_C), F32),
            pltpu.VMEM((GC * REP, DK_C, DV_C), F32),
            pltpu.VMEM((GC * REP, 8, LANES), F32),
        ],
        compiler_params=pltpu.CompilerParams(
            dimension_semantics=("parallel", "arbitrary"), vmem_limit_bytes=VMEM_LIMIT),
        name="gated_deltanet",
    )(proj, proj, proj, proj, beta_all, gc_all, conv_w, conv_w, conv_w, out_norm.reshape(1, DV_C))


def _dn_out_kernel(o_in_ref, x_ref, w_ref, o_ref):
    o_ref[...] = x_ref[...] + jnp.dot(o_in_ref[...], w_ref[...], preferred_element_type=F32)


def _dn_out(o2d, x2d, w, tm=1024, tn=1024):
    T, D = x2d.shape
    K = o2d.shape[1]
    return pl.pallas_call(
        _dn_out_kernel,
        grid=(T // tm, D // tn),
        in_specs=[
            pl.BlockSpec((tm, K), lambda i, j: (i, 0)),
            pl.BlockSpec((tm, tn), lambda i, j: (i, j)),
            pl.BlockSpec((K, tn), lambda i, j: (0, j)),
        ],
        out_specs=pl.BlockSpec((tm, tn), lambda i, j: (i, j)),
        out_shape=jax.ShapeDtypeStruct((T, D), F32),
        compiler_params=pltpu.CompilerParams(
            dimension_semantics=("parallel", "arbitrary"), vmem_limit_bytes=VMEM_LIMIT),
        name="dn_out_proj",
    )(o2d, x2d, w)


def _ab_weight(w):
    D = w.shape[0]
    w = w.astype(BF16)
    o = np.cumsum([0, MIX_A, D_C, H_I * D_I, D_I, H_I, H_B * DK_B, H_B * DK_B, MIX_B, MIX_AB])
    q_a, c, q_i, k_i, w_i, q_b, k_b, v_b, gate = [w[:, o[i]:o[i + 1]] for i in range(9)]
    pad_kw = jnp.zeros((D, LANES - D_I - H_I), w.dtype)
    pad_end = jnp.zeros((D, AB_NP - AB_KW - LANES), w.dtype)
    return jnp.concatenate([q_a, v_b, gate, q_i, q_b, k_b, c, k_i, w_i, pad_kw, pad_end], axis=1)


def _dn_weight(w):
    return jnp.pad(w.astype(BF16), ((0, 0), (0, DN_NP - w.shape[1])))


def kernel(x, ab_norm, ab_w_in, ab_kv_norm, ab_w_uk, ab_w_uv, ab_q_norm, ab_k_norm, ab_ret_norm, ab_w_out,
           dn_norm, dn_w_in, dn_conv, dn_a_log, dn_dt_bias, dn_out_norm, dn_w_out):
    B, S, D = x.shape
    T = B * S
    depth = ab_norm.shape[0] + dn_norm.shape[0]
    h = x.astype(F32).reshape(T, D)
    for layer in range(depth):
        j = layer // 2
        if layer % 2 == 0:
            proj = _norm_proj(h, ab_norm[j].reshape(1, D), _ab_weight(ab_w_in[j]), tm=1024, tn=1024)
            proj3 = proj.reshape(B, S, AB_NP)
            a_out = _dsa(proj3, ab_kv_norm[j].reshape(1, D_C), ab_w_uk[j].astype(BF16),
                         ab_k_norm[j].reshape(1, DH_A), ab_q_norm[j].reshape(1, DH_A), ab_w_uv[j].astype(BF16))
            b_out = _retention(proj3, ab_ret_norm[j])
            h = _ab_out(a_out.reshape(T, MIX_A), b_out.reshape(T, MIX_B), h, ab_w_out[j].astype(BF16))
        else:
            proj = _norm_proj(h, dn_norm[j].reshape(1, D), _dn_weight(dn_w_in[j]), tm=1024, tn=1280)
            o = _gdn(proj.reshape(B, S, DN_NP), dn_conv[j], dn_a_log[j], dn_dt_bias[j], dn_out_norm[j])
            h = _dn_out(o.reshape(T, MIX_C), h, dn_w_out[j].astype(BF16))
    return h.reshape(B, S, D).astype(x.dtype)
```

```python
import functools

import numpy as np
import jax
import jax.numpy as jnp
from jax import lax
from jax.experimental import pallas as pl
from jax.experimental.pallas import tpu as pltpu

F32 = jnp.float32
BF16 = jnp.bfloat16
I32 = jnp.int32
I16 = jnp.int16

EPS = 1e-6
H_A, DH_A, D_C, H_I, D_I = 8, 128, 256, 8, 64
INDEX_TOPK = 256
H_B, DK_B, DV_B, RET_CHUNK, RET_THETA = 4, 128, 256, 128, 10000.0
H_QK, H_V, DK_C, DV_C, CONV_K, DN_CHUNK = 16, 32, 128, 128, 4, 64

MIX_A = H_A * DH_A
MIX_B = H_B * DV_B
MIX_AB = MIX_A + MIX_B
MIX_C = H_V * DV_C

LANES = 128
INT_MIN = np.int32(-2**31)

AB_QA, AB_VB, AB_GATE, AB_QI, AB_QB, AB_KB, AB_C, AB_KW, AB_NP = 0, 1024, 2048, 4096, 4608, 5120, 5632, 5888, 6144
DN_Q, DN_K, DN_V, DN_Z, DN_BA, DN_NP = 0, 2048, 4096, 8192, 12288, 12800

VMEM_LIMIT = 56 * 1024 * 1024


def _sigmoid(x):
    return 1.0 / (1.0 + jnp.exp(-x))


def _silu(x):
    return x * (0.5 * jnp.tanh(0.5 * x) + 0.5)


def _norm_proj_kernel(x_ref, g_ref, w_ref, o_ref, xn_ref):
    @pl.when(pl.program_id(1) == 0)
    def _():
        x = x_ref[...]
        ms = jnp.mean(x * x, axis=-1, keepdims=True)
        xn_ref[...] = (x * lax.rsqrt(ms + EPS) * g_ref[...]).astype(BF16)

    o_ref[...] = jnp.dot(xn_ref[...], w_ref[...], preferred_element_type=F32).astype(o_ref.dtype)


def _norm_proj(x2d, gain, w, tm, tn, out_dtype=F32):
    T, D = x2d.shape
    N = w.shape[1]
    return pl.pallas_call(
        _norm_proj_kernel,
        grid=(T // tm, N // tn),
        in_specs=[
            pl.BlockSpec((tm, D), lambda i, j: (i, 0)),
            pl.BlockSpec((1, D), lambda i, j: (0, 0)),
            pl.BlockSpec((D, tn), lambda i, j: (0, j)),
        ],
        out_specs=pl.BlockSpec((tm, tn), lambda i, j: (i, j)),
        out_shape=jax.ShapeDtypeStruct((T, N), out_dtype),
        scratch_shapes=[pltpu.VMEM((tm, D), BF16)],
        compiler_params=pltpu.CompilerParams(
            dimension_semantics=("parallel", "arbitrary"), vmem_limit_bytes=VMEM_LIMIT),
        name="norm_proj",
    )(x2d, gain, w)


def _dsa_kernel(qa_ref, c_ref, qi_ref, kwq_ref, kwk_ref, gate_ref, kvn_ref, wuk_ref, kn_ref, qn_ref, wuv_ref,
                o_ref, cnT_s, k_s, ki_s, key_s, khi_s, klo_s, bias_s, lg_s, j_s, qall_s, lat_s, *, S, Q, topk):
    qb = pl.program_id(1)
    nkb = qb + 1
    RB = 128
    nrb = nkb * (Q // RB)
    NT = (((1,), (1,)), ((), ()))

    @pl.when(qb == 0)
    def _prep():
        def body(j, _):
            r0 = pl.multiple_of(j * Q, Q)
            c = c_ref[pl.ds(r0, Q), :]
            cn = c * lax.rsqrt(jnp.mean(c * c, axis=-1, keepdims=True) + EPS) * kvn_ref[...]
            kk = jnp.dot(cn.astype(BF16), wuk_ref[...], preferred_element_type=F32)
            kk = kk * lax.rsqrt(jnp.mean(kk * kk, axis=-1, keepdims=True) + EPS) * kn_ref[...]
            cnT_s[j] = cn.T.astype(BF16)
            k_s[pl.ds(r0, Q), :] = kk.astype(BF16)
            ki_s[pl.ds(r0, Q), :] = kwk_ref[pl.ds(r0, Q), 0:D_I].astype(BF16)
            return 0
        lax.fori_loop(0, S // Q, body, 0)

    tpos = qb * Q + lax.broadcasted_iota(I32, (1, Q), 1)
    w_t = kwq_ref[...].T[D_I:D_I + H_I, :] * (H_I ** -0.5 * D_I ** -0.5)

    qi_heads = [qi_ref[:, h * D_I:(h + 1) * D_I].astype(BF16) for h in range(H_I)]

    def score_blk(j, _):
        r0 = pl.multiple_of(j * Q, Q)
        kib = ki_s[pl.ds(r0, Q), :]
        acc = jnp.zeros((Q, Q), F32)
        for h in range(H_I):
            s = lax.dot_general(kib, qi_heads[h], NT, preferred_element_type=F32)
            acc = acc + jnp.maximum(s, 0.0) * w_t[h:h + 1, :]
        acc = acc + 0.0
        bits = pltpu.bitcast(acc, I32)
        key = bits ^ ((bits >> 31) & np.int32(0x7FFFFFFF))
        kpos = r0 + lax.broadcasted_iota(I32, (Q, Q), 0)
        key = jnp.where(kpos <= tpos, key, INT_MIN)
        key_s[pl.ds(r0, Q), :] = key
        khi_s[pl.ds(r0, Q), :] = (key >> 16).astype(I16)
        return 0
    lax.fori_loop(0, nkb, score_blk, 0)

    def count(pred):
        def body(i, acc):
            r0 = pl.multiple_of(i * RB, RB)
            kpos = r0 + lax.broadcasted_iota(I32, (RB, Q), 0)
            return acc + jnp.where(pred(key_s[pl.ds(r0, RB), :], kpos), 1.0, 0.0)
        acc = lax.fori_loop(0, nrb, body, jnp.zeros((RB, Q), F32))
        return jnp.sum(acc, axis=0, keepdims=True)

    def count16(ref, pred):
        def body(j, acc):
            r0 = pl.multiple_of(j * Q, Q)
            return acc + jnp.where(pred(ref[pl.ds(r0, Q), :]), np.int16(1), np.int16(0))
        acc = lax.fori_loop(0, nkb, body, jnp.zeros((Q, Q), I16))
        acc = jnp.sum(acc.reshape(Q // 16, 16, Q), axis=0)
        return jnp.sum(acc.astype(F32), axis=0, keepdims=True)

    def search16(ref, base):
        def bit_body(i, prefix_u):
            cand_u = prefix_u | jnp.left_shift(np.int32(1), 15 - i)
            cand_s = (cand_u - HALF16).astype(I16)
            cnt = base + count16(ref, lambda k: k >= cand_s)
            return jnp.where(cnt >= float(topk), cand_u, prefix_u)
        return lax.fori_loop(0, 16, bit_body, jnp.zeros((1, Q), I32))

    HALF16 = np.int32(1 << 15)
    th = search16(khi_s, 0.0) - HALF16
    th16 = th.astype(I16)
    cnt_hi_gt = count16(khi_s, lambda k: k > th16)

    def lo_blk(j, _):
        r0 = pl.multiple_of(j * Q, Q)
        k = key_s[pl.ds(r0, Q), :]
        lo = (k & np.int32(0xFFFF)) - HALF16
        klo_s[pl.ds(r0, Q), :] = jnp.where((k >> 16) == th, lo, -HALF16).astype(I16)
        return 0
    lax.fori_loop(0, nkb, lo_blk, 0)
    tl_u = search16(klo_s, cnt_hi_gt)
    thr = (th << 16) | tl_u

    tl16 = (tl_u - HALF16).astype(I16)
    cnt_ge = cnt_hi_gt + count16(klo_s, lambda k: k >= tl16)
    ambiguous = (cnt_ge != float(topk)) & (tpos >= topk - 1)
    j_s[...] = jnp.full((1, Q), S, I32)

    @pl.when(jnp.max(jnp.where(ambiguous, 1.0, 0.0)) > 0.0)
    def _ties():
        need = float(topk) - count(lambda k, _: k > thr)
        nbits = int(S).bit_length() - 1

        def jbit(i, prefix):
            cand = prefix | jnp.left_shift(np.int32(1), nbits - 1 - i)
            cnt = count(lambda k, kpos: (k == thr) & (kpos < cand))
            return jnp.where(cnt < need, cand, prefix)
        jstar = lax.fori_loop(0, nbits, jbit, jnp.zeros((1, Q), I32))
        j_s[...] = jnp.where(ambiguous, jstar, S)

    jsel = j_s[...]

    def bias_blk(i, _):
        r0 = pl.multiple_of(i * RB, RB)
        k = key_s[pl.ds(r0, RB), :]
        kpos = r0 + lax.broadcasted_iota(I32, (RB, Q), 0)
        sel = (kpos <= tpos) & ((k > thr) | ((k == thr) & (kpos <= jsel)))
        bias_s[pl.ds(r0, RB), :] = jnp.where(sel, 0.0, -jnp.inf)
        return 0
    lax.fori_loop(0, nrb, bias_blk, 0)

    for h in range(H_A):
        qh = qa_ref[:, h * DH_A:(h + 1) * DH_A]
        qh = qh * lax.rsqrt(jnp.mean(qh * qh, axis=-1, keepdims=True) + EPS) * qn_ref[...] * (DH_A ** -0.5)
        qall_s[h * Q:(h + 1) * Q, :] = qh.astype(BF16)

    def pass_a(j, m):
        r0 = pl.multiple_of(j * Q, Q)
        s = lax.dot_general(k_s[pl.ds(r0, Q), :], qall_s[...], NT, preferred_element_type=F32)
        bias = bias_s[pl.ds(r0, Q), :]
        ms = []
        for h in range(H_A):
            sh = s[:, h * Q:(h + 1) * Q] + bias
            lg_s[pl.ds(r0, Q), h * Q:(h + 1) * Q] = sh
            ms.append(jnp.max(sh, axis=0, keepdims=True))
        return jnp.maximum(m, jnp.concatenate(ms, axis=1))
    m = lax.fori_loop(0, nkb, pass_a, jnp.full((1, H_A * Q), -jnp.inf, F32))

    lat_s[...] = jnp.zeros_like(lat_s)

    def pass_b(j, l):
        r0 = pl.multiple_of(j * Q, Q)
        p = jnp.exp(lg_s[pl.ds(r0, Q), :] - m)
        lat_s[...] += jnp.dot(cnT_s[j], p.astype(BF16), preferred_element_type=F32)
        return l + jnp.sum(p, axis=0, keepdims=True)
    l = lax.fori_loop(0, nkb, pass_b, jnp.zeros((1, H_A * Q), F32))

    lat = (lat_s[...] * (1.0 / l)).astype(BF16)
    for h in range(H_A):
        cols = slice(h * DH_A, (h + 1) * DH_A)
        out_h = lax.dot_general(lat[:, h * Q:(h + 1) * Q], wuv_ref[h], (((0,), (0,)), ((), ())),
                                preferred_element_type=F32)
        o_ref[:, cols] = (out_h * _silu(gate_ref[:, cols])).astype(o_ref.dtype)


def _dsa(proj, kv_norm, w_uk, k_norm, q_norm, w_uv, Q=256):
    B, S, _ = proj.shape
    topk = min(INDEX_TOPK, S // 4)
    assert S % Q == 0 and topk <= Q and (S & (S - 1)) == 0
    kern = functools.partial(_dsa_kernel, S=S, Q=Q, topk=topk)
    full = lambda *shape: pl.BlockSpec(shape, lambda b, q: (0,) * len(shape))
    return pl.pallas_call(
        kern,
        grid=(B, S // Q),
        in_specs=[
            pl.BlockSpec((None, Q, MIX_A), lambda b, q: (b, q, AB_QA // MIX_A)),
            pl.BlockSpec((None, S, D_C), lambda b, q: (b, 0, AB_C // D_C)),
            pl.BlockSpec((None, Q, H_I * D_I), lambda b, q: (b, q, AB_QI // (H_I * D_I))),
            pl.BlockSpec((None, Q, LANES), lambda b, q: (b, q, AB_KW // LANES)),
            pl.BlockSpec((None, S, LANES), lambda b, q: (b, 0, AB_KW // LANES)),
            pl.BlockSpec((None, Q, MIX_A), lambda b, q: (b, q, AB_GATE // MIX_A)),
            full(1, D_C), full(D_C, DH_A), full(1, DH_A), full(1, DH_A), full(H_A, D_C, DH_A),
        ],
        out_specs=pl.BlockSpec((None, Q, MIX_A), lambda b, q: (b, q, 0)),
        out_shape=jax.ShapeDtypeStruct((B, S, MIX_A), BF16),
        scratch_shapes=[
            pltpu.VMEM((S // Q, D_C, Q), BF16),
            pltpu.VMEM((S, DH_A), BF16),
            pltpu.VMEM((S, D_I), BF16),
            pltpu.VMEM((S, Q), I32),
            pltpu.VMEM((S, Q), I16),
            pltpu.VMEM((S, Q), I16),
            pltpu.VMEM((S, Q), F32),
            pltpu.VMEM((S, H_A * Q), F32),
            pltpu.VMEM((1, Q), I32),
            pltpu.VMEM((H_A * Q, DH_A), BF16),
            pltpu.VMEM((D_C, H_A * Q), F32),
        ],
        compiler_params=pltpu.CompilerParams(
            dimension_semantics=("parallel", "arbitrary"), vmem_limit_bytes=VMEM_LIMIT),
        name="dsa_attention",
    )(proj, proj, proj, proj, proj, proj, kv_norm, w_uk, k_norm, q_norm, w_uv)


def _ret_kernel(lg_ref, q_ref, k_ref, v_ref, gate_ref, cos_ref, sin_ref, g_ref, o_ref, *, S, C):
    h = pl.program_id(1)
    lg = lg_ref[h]
    NT = (((1,), (1,)), ((), ()))
    TN = (((0,), (0,)), ((), ()))
    N = S // C
    ii = lax.broadcasted_iota(I32, (C, C), 0)
    jj = lax.broadcasted_iota(I32, (C, C), 1)
    diff = (ii - jj).astype(F32)
    dmask = jnp.where(diff >= 0, jnp.exp(jnp.maximum(diff, 0.0) * lg), 0.0)
    icol = lax.broadcasted_iota(I32, (C, 1), 0).astype(F32)
    k_dec = jnp.exp((C - 1 - icol) * lg)
    q_dec = jnp.exp((icol + 1.0) * lg)
    chunk_decay = jnp.exp(jnp.full((1, DV_B), float(C), F32) * lg)

    def rot(x, cos, sin):
        return x * cos + pltpu.roll(x, DK_B // 2, axis=1) * sin

    q, k, v = [], [], []
    for n in range(N):
        rows = slice(n * C, (n + 1) * C)
        cos, sin = cos_ref[rows, :], sin_ref[rows, :]
        q.append(rot(q_ref[rows, :], cos, sin))
        k.append(rot(k_ref[rows, :], cos, sin) * (DK_B ** -0.5))
        v.append(v_ref[rows, :])
    inner = [lax.dot_general(q[n], k[n], NT, preferred_element_type=F32) * dmask for n in range(N)]
    kv = [lax.dot_general(k[n] * k_dec, v[n], TN, preferred_element_type=F32) for n in range(N - 1)]
    out = [jnp.dot(inner[n], v[n], preferred_element_type=F32) for n in range(N)]
    st = kv[0]
    for n in range(1, N):
        out[n] = out[n] + jnp.dot(q[n] * q_dec, st, preferred_element_type=F32)
        if n + 1 < N:
            st = st * chunk_decay + kv[n]
    for n in range(N):
        rows = slice(n * C, (n + 1) * C)
        o = out[n] * lax.rsqrt(jnp.mean(out[n] * out[n], axis=-1, keepdims=True) + EPS) * g_ref[...]
        o_ref[rows, :] = (o * _silu(gate_ref[rows, :])).astype(o_ref.dtype)


def _retention(proj, ret_norm):
    B, S, _ = proj.shape
    C = RET_CHUNK
    pos = jnp.arange(S, dtype=F32)
    inv_freq = 1.0 / (RET_THETA ** jnp.linspace(0.0, 1.0, DK_B // 2, dtype=F32))
    ang = pos[:, None] * inv_freq[None, :]
    cos2 = jnp.concatenate([jnp.cos(ang), jnp.cos(ang)], axis=-1)
    sin2 = jnp.concatenate([-jnp.sin(ang), jnp.sin(ang)], axis=-1)
    log_gamma = jnp.log1p(-jnp.exp2(-5.0 - jnp.arange(H_B, dtype=F32)))
    kern = functools.partial(_ret_kernel, S=S, C=C)
    return pl.pallas_call(
        kern,
        grid_spec=pltpu.PrefetchScalarGridSpec(
            num_scalar_prefetch=1,
            grid=(B, H_B),
            in_specs=[
                pl.BlockSpec((None, S, DK_B), lambda b, h, lg: (b, 0, AB_QB // DK_B + h)),
                pl.BlockSpec((None, S, DK_B), lambda b, h, lg: (b, 0, AB_KB // DK_B + h)),
                pl.BlockSpec((None, S, DV_B), lambda b, h, lg: (b, 0, AB_VB // DV_B + h)),
                pl.BlockSpec((None, S, DV_B), lambda b, h, lg: (b, 0, (AB_GATE + MIX_A) // DV_B + h)),
                pl.BlockSpec((S, DK_B), lambda b, h, lg: (0, 0)),
                pl.BlockSpec((S, DK_B), lambda b, h, lg: (0, 0)),
                pl.BlockSpec((None, 1, DV_B), lambda b, h, lg: (h, 0, 0)),
            ],
            out_specs=pl.BlockSpec((None, S, DV_B), lambda b, h, lg: (b, 0, h)),
        ),
        out_shape=jax.ShapeDtypeStruct((B, S, MIX_B), BF16),
        compiler_params=pltpu.CompilerParams(
            dimension_semantics=("parallel", "arbitrary"), vmem_limit_bytes=VMEM_LIMIT),
        name="retention",
    )(log_gamma, proj, proj, proj, proj, cos2, sin2, ret_norm.reshape(H_B, 1, DV_B))


def _ab_out_kernel(a_ref, b_ref, x_ref, w_ref, o_ref):
    o_ref[...] = (x_ref[...] + jnp.dot(a_ref[...], w_ref[:MIX_A, :], preferred_element_type=F32)
                  + jnp.dot(b_ref[...], w_ref[MIX_A:, :], preferred_element_type=F32))


def _ab_out(a2d, b2d, x2d, w, tm=1024, tn=1024):
    T, D = x2d.shape
    return pl.pallas_call(
        _ab_out_kernel,
        grid=(T // tm, D // tn),
        in_specs=[
            pl.BlockSpec((tm, MIX_A), lambda i, j: (i, 0)),
            pl.BlockSpec((tm, MIX_B), lambda i, j: (i, 0)),
            pl.BlockSpec((tm, tn), lambda i, j: (i, j)),
            pl.BlockSpec((MIX_AB, tn), lambda i, j: (0, j)),
        ],
        out_specs=pl.BlockSpec((tm, tn), lambda i, j: (i, j)),
        out_shape=jax.ShapeDtypeStruct((T, D), F32),
        compiler_params=pltpu.CompilerParams(
            dimension_semantics=("parallel", "arbitrary"), vmem_limit_bytes=VMEM_LIMIT),
        name="ab_out_proj",
    )(a2d, b2d, x2d, w)


def _gdn_gates_kernel(ba_ref, alog_ref, dt_ref, beta_ref, gc_ref, *, S, C):
    ba = ba_ref[...].astype(F32)
    beta_ref[...] = _sigmoid(ba)
    xg = ba + dt_ref[...]
    softplus = jnp.maximum(xg, 0.0) + jnp.log(1.0 + jnp.exp(-jnp.abs(xg)))
    g_all = -jnp.exp(alog_ref[...]) * softplus
    rc = lax.broadcasted_iota(I32, (S, LANES), 0) & (C - 1)
    sh = 1
    while sh < C:
        g_all = g_all + jnp.where(rc >= sh, pltpu.roll(g_all, sh, axis=0), 0.0)
        sh *= 2
    gc_ref[...] = g_all


def _gdn_gates(proj, a_log, dt_bias):
    B, S, _ = proj.shape
    alog_row = jnp.zeros((1, LANES), F32).at[0, H_V:2 * H_V].set(a_log)
    dt_row = jnp.zeros((1, LANES), F32).at[0, H_V:2 * H_V].set(dt_bias)
    row = pl.BlockSpec((1, LANES), lambda b: (0, 0))
    out = pl.BlockSpec((None, S, LANES), lambda b: (b, 0, 0))
    return pl.pallas_call(
        functools.partial(_gdn_gates_kernel, S=S, C=DN_CHUNK),
        grid=(B,),
        in_specs=[pl.BlockSpec((None, S, LANES), lambda b: (b, 0, DN_BA // LANES)), row, row],
        out_specs=[out, out],
        out_shape=[jax.ShapeDtypeStruct((B, S, LANES), F32)] * 2,
        compiler_params=pltpu.CompilerParams(dimension_semantics=("parallel",), vmem_limit_bytes=VMEM_LIMIT),
        name="gdn_gates",
    )(proj, alog_row, dt_row)


def _gdn_kernel(q_ref, k_ref, v_ref, z_ref, beta_ref, gc_ref, cwq_ref, cwk_ref, cwv_ref, on_ref,
                o_ref, qn_s, kn_s, va_s, bb_s, gb_s, lhs_s, au_s, ku_s, gl_s, *, S, C, GC, U):
    g = pl.program_id(1)
    NG = S // (C * GC)
    REP = H_V // H_QK
    VW = REP * DV_C
    NT = (((1,), (1,)), ((), ()))
    TN = (((0,), (0,)), ((), ()))

    GR = GC * C
    HALO = 16

    def prep(gi):
        u, r0 = gi // NG, (gi % NG) * GR

        def conv_silu(ref, w, width):
            cols = slice(u * width, (u + 1) * width)
            w = w[:, cols]
            if r0 == 0:
                xw = jnp.concatenate([jnp.zeros((HALO, width), F32), ref[0:GR, cols].astype(F32)], axis=0)
            else:
                xw = ref[r0 - HALO:r0 + GR, cols].astype(F32)
            y = xw[HALO:] * w[CONV_K - 1:CONV_K, :]
            for d in range(1, CONV_K):
                y = y + pltpu.roll(xw, d, axis=0)[HALO:] * w[CONV_K - 1 - d:CONV_K - d, :]
            return _silu(y)

        def l2n(x):
            return x * lax.rsqrt(jnp.sum(x * x, axis=-1, keepdims=True) + EPS)

        qn_s[u, r0:r0 + GR, :] = l2n(conv_silu(q_ref, cwq_ref[...], DK_C)) * (DK_C ** -0.5)
        kn_s[u, r0:r0 + GR, :] = l2n(conv_silu(k_ref, cwk_ref[...], DK_C))
        va_s[u, r0:r0 + GR, :] = conv_silu(v_ref, cwv_ref[...], VW)
        lane = lax.broadcasted_iota(I32, (GR, LANES), 1)
        for j in range(REP):
            hv = (g * U + u) * REP + j
            bcol = jnp.sum(jnp.where(lane == hv, beta_ref[r0:r0 + GR, :], 0.0), axis=-1, keepdims=True)
            gcol = jnp.sum(jnp.where(lane == hv + H_V, gc_ref[r0:r0 + GR, :], 0.0), axis=-1, keepdims=True)
            bb_s[u, j, r0:r0 + GR, :] = jnp.broadcast_to(bcol, (GR, LANES))
            gb_s[u, j, r0:r0 + GR, :] = jnp.broadcast_to(gcol, (GR, LANES))

    assert REP * C == LANES
    ii = lax.broadcasted_iota(I32, (REP * C, REP * C), 0)
    jj = lax.broadcasted_iota(I32, (REP * C, REP * C), 1)
    same_head = (ii >= C) == (jj >= C)
    tri = same_head & (ii >= jj)
    strict = same_head & (ii > jj)
    eye = jnp.where(ii == jj, 1.0, 0.0)

    dot = functools.partial(jnp.dot, preferred_element_type=F32)
    chains = [(c, j) for c in range(GC) for j in range(REP)]

    def row0(gi, c):
        return ((gi % NG) * GC + c) * C

    def scan_step(gi, c, states):
        u, r0 = gi // NG, row0(gi, c)
        for j in range(REP):
            i = c * REP + j
            cols = slice(u * VW + j * DV_C, u * VW + (j + 1) * DV_C)
            r = dot(lhs_s[i], states[j].astype(BF16))
            o = r[DK_C:] + au_s[i]
            states[j] = states[j] * gl_s[i, 0:1, :] + ku_s[i] - r[:DK_C]
            o = o * lax.rsqrt(jnp.mean(o * o, axis=-1, keepdims=True) + EPS) * on_ref[...]
            o = o * _silu(z_ref[r0:r0 + C, cols].astype(F32))
            o_ref[r0:r0 + C, cols] = o.astype(o_ref.dtype)

    def stage(gi, pending):
        pending = list(pending)

        def tick():
            if pending:
                pending.pop(0)()
        u = gi // NG
        r0s = [row0(gi, c) for c in range(GC)]
        qc = [qn_s[u, r0s[c]:r0s[c] + C, :] for c in range(GC)]
        kc = [kn_s[u, r0s[c]:r0s[c] + C, :] for c in range(GC)]
        qkk = [lax.dot_general(jnp.concatenate([qc[c], kc[c]], axis=0).astype(BF16),
                               jnp.concatenate([kc[c], kc[c]], axis=0).astype(BF16),
                               NT, preferred_element_type=F32) for c in range(GC)]
        tick()
        gcb, bb, p, attn = {}, {}, [], []
        for c in range(GC):
            for j in range(REP):
                gcb[c, j] = gb_s[u, j, r0s[c]:r0s[c] + C, :]
                bb[c, j] = bb_s[u, j, r0s[c]:r0s[c] + C, :]
            g2 = jnp.concatenate([gcb[c, j] for j in range(REP)], axis=0)
            b2 = jnp.concatenate([bb[c, j] for j in range(REP)], axis=0)
            gcr = jnp.sum(jnp.where(ii == jj, g2, 0.0), axis=0, keepdims=True)
            decay = jnp.exp(jnp.where(tri, g2 - gcr, -jnp.inf))
            kk2 = jnp.concatenate([qkk[c][C:]] * REP, axis=0)
            qk2 = jnp.concatenate([qkk[c][:C]] * REP, axis=0)
            p.append(jnp.where(strict, -(kk2 * b2 * decay), 0.0))
            attn.append(qk2 * decay)
        qm = [dot(p[c].astype(BF16), p[c].astype(BF16)) for c in range(GC)]
        tick()
        sm = [eye + p[c] for c in range(GC)]
        for _ in range(4):
            r = [dot(qm[c].astype(BF16), jnp.concatenate([qm[c], sm[c]], axis=1).astype(BF16)) for c in range(GC)]
            tick()
            qm = [r[c][:, :REP * C] for c in range(GC)]
            sm = [sm[c] + r[c][:, REP * C:] for c in range(GC)]
        tinv = [sm[c] + dot(qm[c].astype(BF16), sm[c].astype(BF16)) for c in range(GC)]
        tick()
        uw2, glast = [], {}
        for c in range(GC):
            rhs = []
            for j in range(REP):
                vb = va_s[u, r0s[c]:r0s[c] + C, j * DV_C:(j + 1) * DV_C] * bb[c, j]
                kbg = kc[c] * bb[c, j] * jnp.exp(gcb[c, j])
                rhs.append(jnp.concatenate([vb, kbg], axis=1))
                glast[c, j] = gcb[c, j][C - 1:C, :]
            uw2.append(dot(tinv[c].astype(BF16), jnp.concatenate(rhs, axis=0).astype(BF16)))
        tick()
        uw = {(c, j): uw2[c][j * C:(j + 1) * C] for c, j in chains}
        ktuw = {(c, j): lax.dot_general(kc[c] * jnp.exp(glast[c, j] - gcb[c, j]), uw[c, j], TN,
                                        preferred_element_type=F32) for c, j in chains}
        tick()
        atuw2 = [dot(attn[c].astype(BF16), uw2[c].astype(BF16)) for c in range(GC)]
        atuw = {(c, j): atuw2[c][j * C:(j + 1) * C] for c, j in chains}
        while pending:
            tick()
        for c, j in chains:
            i = c * REP + j
            lhs_s[i] = jnp.concatenate([ktuw[c, j][:, DV_C:], qc[c] * jnp.exp(gcb[c, j]) - atuw[c, j][:, DV_C:]],
                                       axis=0).astype(BF16)
            au_s[i] = atuw[c, j][:, :DV_C]
            ku_s[i] = ktuw[c, j][:, :DV_C]
            gl_s[i] = jnp.broadcast_to(jnp.exp(glast[c, j]), (8, LANES))

    states = [[jnp.zeros((DK_C, DV_C), F32) for _ in range(REP)] for _ in range(U)]
    prep(0)
    for gi in range(U * NG):
        if gi + 1 < U * NG:
            prep(gi + 1)
        stage(gi, [functools.partial(scan_step, gi - 1, c, states[(gi - 1) // NG]) for c in range(GC)] if gi else [])
    for c in range(GC):
        scan_step(U * NG - 1, c, states[U - 1])


def _gdn(proj, conv_w, a_log, dt_bias, out_norm):
    B, S, _ = proj.shape
    C = DN_CHUNK
    REP = H_V // H_QK
    VW = REP * DV_C
    GC = 8
    U = 2
    assert S % (C * GC) == 0 and H_QK % U == 0
    beta_all, gc_all = _gdn_gates(proj, a_log, dt_bias)
    kern = functools.partial(_gdn_kernel, S=S, C=C, GC=GC, U=U)
    KW, UW = U * DK_C, U * VW
    return pl.pallas_call(
        kern,
        grid=(B, H_QK // U),
        in_specs=[
            pl.BlockSpec((None, S, KW), lambda b, g: (b, 0, DN_Q // KW + g)),
            pl.BlockSpec((None, S, KW), lambda b, g: (b, 0, DN_K // KW + g)),
            pl.BlockSpec((None, S, UW), lambda b, g: (b, 0, DN_V // UW + g)),
            pl.BlockSpec((None, S, UW), lambda b, g: (b, 0, DN_Z // UW + g)),
            pl.BlockSpec((None, S, LANES), lambda b, g: (b, 0, 0)),
            pl.BlockSpec((None, S, LANES), lambda b, g: (b, 0, 0)),
            pl.BlockSpec((CONV_K, KW), lambda b, g: (0, DN_Q // KW + g)),
            pl.BlockSpec((CONV_K, KW), lambda b, g: (0, DN_K // KW + g)),
            pl.BlockSpec((CONV_K, UW), lambda b, g: (0, DN_V // UW + g)),
            pl.BlockSpec((1, DV_C), lambda b, g: (0, 0)),
        ],
        out_specs=pl.BlockSpec((None, S, UW), lambda b, g: (b, 0, g)),
        out_shape=jax.ShapeDtypeStruct((B, S, MIX_C), BF16),
        scratch_shapes=[
            pltpu.VMEM((U, S, DK_C), F32),
            pltpu.VMEM((U, S, DK_C), F32),
            pltpu.VMEM((U, S, VW), F32),
            pltpu.VMEM((U, REP, S, LANES), F32),
            pltpu.VMEM((U, REP, S, LANES), F32),
            pltpu.VMEM((GC * REP, DK_C + C, DK_C), BF16),
            pltpu.VMEM((GC * REP, C, DV_C), F32),
            pltpu.VMEM((GC * REP, DK_C, DV_C), F32),
            pltpu.VMEM((GC * REP, 8, LANES), F32),
        ],
        compiler_params=pltpu.CompilerParams(
            dimension_semantics=("parallel", "arbitrary"), vmem_limit_bytes=VMEM_LIMIT),
        name="gated_deltanet",
    )(proj, proj, proj, proj, beta_all, gc_all, conv_w, conv_w, conv_w, out_norm.reshape(1, DV_C))


def _dn_out_kernel(o_in_ref, x_ref, w_ref, o_ref):
    o_ref[...] = x_ref[...] + jnp.dot(o_in_ref[...], w_ref[...], preferred_element_type=F32)


def _dn_out(o2d, x2d, w, tm=1024, tn=1024):
    T, D = x2d.shape
    K = o2d.shape[1]
    return pl.pallas_call(
        _dn_out_kernel,
        grid=(T // tm, D // tn),
        in_specs=[
            pl.BlockSpec((tm, K), lambda i, j: (i, 0)),
            pl.BlockSpec((tm, tn), lambda i, j: (i, j)),
            pl.BlockSpec((K, tn), lambda i, j: (0, j)),
        ],
        out_specs=pl.BlockSpec((tm, tn), lambda i, j: (i, j)),
        out_shape=jax.ShapeDtypeStruct((T, D), F32),
        compiler_params=pltpu.CompilerParams(
            dimension_semantics=("parallel", "arbitrary"), vmem_limit_bytes=VMEM_LIMIT),
        name="dn_out_proj",
    )(o2d, x2d, w)


def _ab_weight(w):
    D = w.shape[0]
    w = w.astype(BF16)
    o = np.cumsum([0, MIX_A, D_C, H_I * D_I, D_I, H_I, H_B * DK_B, H_B * DK_B, MIX_B, MIX_AB])
    q_a, c, q_i, k_i, w_i, q_b, k_b, v_b, gate = [w[:, o[i]:o[i + 1]] for i in range(9)]
    pad_kw = jnp.zeros((D, LANES - D_I - H_I), w.dtype)
    pad_end = jnp.zeros((D, AB_NP - AB_KW - LANES), w.dtype)
    return jnp.concatenate([q_a, v_b, gate, q_i, q_b, k_b, c, k_i, w_i, pad_kw, pad_end], axis=1)


def _dn_weight(w):
    return jnp.pad(w.astype(BF16), ((0, 0), (0, DN_NP - w.shape[1])))


def kernel(x, ab_norm, ab_w_in, ab_kv_norm, ab_w_uk, ab_w_uv, ab_q_norm, ab_k_norm, ab_ret_norm, ab_w_out,
           dn_norm, dn_w_in, dn_conv, dn_a_log, dn_dt_bias, dn_out_norm, dn_w_out):
    B, S, D = x.shape
    T = B * S
    depth = ab_norm.shape[0] + dn_norm.shape[0]
    h = x.astype(F32).reshape(T, D)
    for layer in range(depth):
        j = layer // 2
        if layer % 2 == 0:
            proj = _norm_proj(h, ab_norm[j].reshape(1, D), _ab_weight(ab_w_in[j]), tm=1024, tn=1536)
            proj3 = proj.reshape(B, S, AB_NP)
            a_out = _dsa(proj3, ab_kv_norm[j].reshape(1, D_C), ab_w_uk[j].astype(BF16),
                         ab_k_norm[j].reshape(1, DH_A), ab_q_norm[j].reshape(1, DH_A), ab_w_uv[j].astype(BF16))
            b_out = _retention(proj3, ab_ret_norm[j])
            h = _ab_out(a_out.reshape(T, MIX_A), b_out.reshape(T, MIX_B), h, ab_w_out[j].astype(BF16))
        else:
            proj = _norm_proj(h, dn_norm[j].reshape(1, D), _dn_weight(dn_w_in[j]), tm=1024, tn=1280, out_dtype=BF16)
            o = _gdn(proj.reshape(B, S, DN_NP), dn_conv[j], dn_a_log[j], dn_dt_bias[j], dn_out_norm[j])
            h = _dn_out(o.reshape(T, MIX_C), h, dn_w_out[j].astype(BF16))
    return h.reshape(B, S, D).astype(x.dtype)
```

```python
import functools

import numpy as np
import jax
import jax.numpy as jnp
from jax import lax
from jax.experimental import pallas as pl
from jax.experimental.pallas import tpu as pltpu

F32 = jnp.float32
BF16 = jnp.bfloat16
I32 = jnp.int32
I16 = jnp.int16

EPS = 1e-6
H_A, DH_A, D_C, H_I, D_I = 8, 128, 256, 8, 64
INDEX_TOPK = 256
H_B, DK_B, DV_B, RET_CHUNK, RET_THETA = 4, 128, 256, 128, 10000.0
H_QK, H_V, DK_C, DV_C, CONV_K, DN_CHUNK = 16, 32, 128, 128, 4, 64

MIX_A = H_A * DH_A
MIX_B = H_B * DV_B
MIX_AB = MIX_A + MIX_B
MIX_C = H_V * DV_C

LANES = 128
INT_MIN = np.int32(-2**31)

AB_QA, AB_VB, AB_GATE, AB_QI, AB_QB, AB_KB, AB_C, AB_KW, AB_NP = 0, 1024, 2048, 4096, 4608, 5120, 5632, 5888, 6144
DN_Q, DN_K, DN_V, DN_Z, DN_BA, DN_NP = 0, 2048, 4096, 8192, 12288, 12800

VMEM_LIMIT = 56 * 1024 * 1024


def _sigmoid(x):
    return 1.0 / (1.0 + jnp.exp(-x))


def _silu(x):
    return x * (0.5 * jnp.tanh(0.5 * x) + 0.5)


def _norm_proj_kernel(x_ref, g_ref, w_ref, o_ref, xn_ref):
    @pl.when(pl.program_id(1) == 0)
    def _():
        x = x_ref[...]
        ms = jnp.mean(x * x, axis=-1, keepdims=True)
        xn_ref[...] = (x * lax.rsqrt(ms + EPS) * g_ref[...]).astype(BF16)

    o_ref[...] = jnp.dot(xn_ref[...], w_ref[...], preferred_element_type=F32).astype(o_ref.dtype)


def _norm_proj(x2d, gain, w, tm, tn, out_dtype=F32):
    T, D = x2d.shape
    N = w.shape[1]
    return pl.pallas_call(
        _norm_proj_kernel,
        grid=(T // tm, N // tn),
        in_specs=[
            pl.BlockSpec((tm, D), lambda i, j: (i, 0)),
            pl.BlockSpec((1, D), lambda i, j: (0, 0)),
            pl.BlockSpec((D, tn), lambda i, j: (0, j)),
        ],
        out_specs=pl.BlockSpec((tm, tn), lambda i, j: (i, j)),
        out_shape=jax.ShapeDtypeStruct((T, N), out_dtype),
        scratch_shapes=[pltpu.VMEM((tm, D), BF16)],
        compiler_params=pltpu.CompilerParams(
            dimension_semantics=("parallel", "arbitrary"), vmem_limit_bytes=VMEM_LIMIT),
        name="norm_proj",
    )(x2d, gain, w)


def _dsa_kernel(qa_ref, c_ref, qi_ref, kwq_ref, kwk_ref, gate_ref, kvn_ref, wuk_ref, kn_ref, qn_ref, wuv_ref,
                o_ref, cnT_s, k_s, ki_s, key_s, khi_s, klo_s, bias_s, lg_s, j_s, qall_s, lat_s, *, S, Q, topk):
    qb = pl.program_id(1)
    nkb = qb + 1
    RB = 128
    nrb = nkb * (Q // RB)
    NT = (((1,), (1,)), ((), ()))

    @pl.when(qb == 0)
    def _prep():
        def body(j, _):
            r0 = pl.multiple_of(j * Q, Q)
            c = c_ref[pl.ds(r0, Q), :]
            cn = c * lax.rsqrt(jnp.mean(c * c, axis=-1, keepdims=True) + EPS) * kvn_ref[...]
            kk = jnp.dot(cn.astype(BF16), wuk_ref[...], preferred_element_type=F32)
            kk = kk * lax.rsqrt(jnp.mean(kk * kk, axis=-1, keepdims=True) + EPS) * kn_ref[...]
            cnT_s[j] = cn.T.astype(BF16)
            k_s[pl.ds(r0, Q), :] = kk.astype(BF16)
            ki_s[pl.ds(r0, Q), :] = kwk_ref[pl.ds(r0, Q), 0:D_I].astype(BF16)
            return 0
        lax.fori_loop(0, S // Q, body, 0)

    tpos = qb * Q + lax.broadcasted_iota(I32, (1, Q), 1)
    w_t = kwq_ref[...].T[D_I:D_I + H_I, :] * (H_I ** -0.5 * D_I ** -0.5)

    qi_heads = [qi_ref[:, h * D_I:(h + 1) * D_I].astype(BF16) for h in range(H_I)]

    def score_blk(j, _):
        r0 = pl.multiple_of(j * Q, Q)
        kib = ki_s[pl.ds(r0, Q), :]
        acc = jnp.zeros((Q, Q), F32)
        for h in range(H_I):
            s = lax.dot_general(kib, qi_heads[h], NT, preferred_element_type=F32)
            acc = acc + jnp.maximum(s, 0.0) * w_t[h:h + 1, :]
        acc = acc + 0.0
        bits = pltpu.bitcast(acc, I32)
        key = bits ^ ((bits >> 31) & np.int32(0x7FFFFFFF))
        kpos = r0 + lax.broadcasted_iota(I32, (Q, Q), 0)
        key = jnp.where(kpos <= tpos, key, INT_MIN)
        key_s[pl.ds(r0, Q), :] = key
        khi_s[pl.ds(r0, Q), :] = (key >> 16).astype(I16)
        return 0
    lax.fori_loop(0, nkb, score_blk, 0)

    def count(pred):
        def body(i, acc):
            r0 = pl.multiple_of(i * RB, RB)
            kpos = r0 + lax.broadcasted_iota(I32, (RB, Q), 0)
            return acc + jnp.where(pred(key_s[pl.ds(r0, RB), :], kpos), 1.0, 0.0)
        acc = lax.fori_loop(0, nrb, body, jnp.zeros((RB, Q), F32))
        return jnp.sum(acc, axis=0, keepdims=True)

    def count16(ref, pred):
        def body(j, acc):
            r0 = pl.multiple_of(j * Q, Q)
            return acc + jnp.where(pred(ref[pl.ds(r0, Q), :]), np.int16(1), np.int16(0))
        acc = lax.fori_loop(0, nkb, body, jnp.zeros((Q, Q), I16))
        acc = jnp.sum(acc.reshape(Q // 16, 16, Q), axis=0)
        return jnp.sum(acc.astype(F32), axis=0, keepdims=True)

    def search16(ref, base):
        def bit_body(i, prefix_u):
            cand_u = prefix_u | jnp.left_shift(np.int32(1), 15 - i)
            cand_s = (cand_u - HALF16).astype(I16)
            cnt = base + count16(ref, lambda k: k >= cand_s)
            return jnp.where(cnt >= float(topk), cand_u, prefix_u)
        return lax.fori_loop(0, 16, bit_body, jnp.zeros((1, Q), I32))

    HALF16 = np.int32(1 << 15)
    th = search16(khi_s, 0.0) - HALF16
    th16 = th.astype(I16)
    cnt_hi_gt = count16(khi_s, lambda k: k > th16)

    def lo_blk(j, _):
        r0 = pl.multiple_of(j * Q, Q)
        k = key_s[pl.ds(r0, Q), :]
        lo = (k & np.int32(0xFFFF)) - HALF16
        klo_s[pl.ds(r0, Q), :] = jnp.where((k >> 16) == th, lo, -HALF16).astype(I16)
        return 0
    lax.fori_loop(0, nkb, lo_blk, 0)
    tl_u = search16(klo_s, cnt_hi_gt)
    thr = (th << 16) | tl_u

    tl16 = (tl_u - HALF16).astype(I16)
    cnt_ge = cnt_hi_gt + count16(klo_s, lambda k: k >= tl16)
    ambiguous = (cnt_ge != float(topk)) & (tpos >= topk - 1)
    j_s[...] = jnp.full((1, Q), S, I32)

    @pl.when(jnp.max(jnp.where(ambiguous, 1.0, 0.0)) > 0.0)
    def _ties():
        need = float(topk) - count(lambda k, _: k > thr)
        nbits = int(S).bit_length() - 1

        def jbit(i, prefix):
            cand = prefix | jnp.left_shift(np.int32(1), nbits - 1 - i)
            cnt = count(lambda k, kpos: (k == thr) & (kpos < cand))
            return jnp.where(cnt < need, cand, prefix)
        jstar = lax.fori_loop(0, nbits, jbit, jnp.zeros((1, Q), I32))
        j_s[...] = jnp.where(ambiguous, jstar, S)

    jsel = j_s[...]

    def bias_blk(i, _):
        r0 = pl.multiple_of(i * RB, RB)
        k = key_s[pl.ds(r0, RB), :]
        kpos = r0 + lax.broadcasted_iota(I32, (RB, Q), 0)
        sel = (kpos <= tpos) & ((k > thr) | ((k == thr) & (kpos <= jsel)))
        bias_s[pl.ds(r0, RB), :] = jnp.where(sel, 0.0, -jnp.inf)
        return 0
    lax.fori_loop(0, nrb, bias_blk, 0)

    for h in range(H_A):
        qh = qa_ref[:, h * DH_A:(h + 1) * DH_A]
        qh = qh * lax.rsqrt(jnp.mean(qh * qh, axis=-1, keepdims=True) + EPS) * qn_ref[...] * (DH_A ** -0.5)
        qall_s[h * Q:(h + 1) * Q, :] = qh.astype(BF16)

    def pass_a(j, m):
        r0 = pl.multiple_of(j * Q, Q)
        s = lax.dot_general(k_s[pl.ds(r0, Q), :], qall_s[...], NT, preferred_element_type=F32)
        bias = bias_s[pl.ds(r0, Q), :]
        ms = []
        for h in range(H_A):
            sh = s[:, h * Q:(h + 1) * Q] + bias
            lg_s[pl.ds(r0, Q), h * Q:(h + 1) * Q] = sh
            ms.append(jnp.max(sh, axis=0, keepdims=True))
        return jnp.maximum(m, jnp.concatenate(ms, axis=1))
    m = lax.fori_loop(0, nkb, pass_a, jnp.full((1, H_A * Q), -jnp.inf, F32))

    lat_s[...] = jnp.zeros_like(lat_s)

    def pass_b(j, l):
        r0 = pl.multiple_of(j * Q, Q)
        p = jnp.exp(lg_s[pl.ds(r0, Q), :] - m)
        lat_s[...] += jnp.dot(cnT_s[j], p.astype(BF16), preferred_element_type=F32)
        return l + jnp.sum(p, axis=0, keepdims=True)
    l = lax.fori_loop(0, nkb, pass_b, jnp.zeros((1, H_A * Q), F32))

    lat = (lat_s[...] * (1.0 / l)).astype(BF16)
    for h in range(H_A):
        cols = slice(h * DH_A, (h + 1) * DH_A)
        out_h = lax.dot_general(lat[:, h * Q:(h + 1) * Q], wuv_ref[h], (((0,), (0,)), ((), ())),
                                preferred_element_type=F32)
        o_ref[:, cols] = (out_h * _silu(gate_ref[:, cols])).astype(o_ref.dtype)


def _dsa(proj, kv_norm, w_uk, k_norm, q_norm, w_uv, Q=256):
    B, S, _ = proj.shape
    topk = min(INDEX_TOPK, S // 4)
    assert S % Q == 0 and topk <= Q and (S & (S - 1)) == 0
    kern = functools.partial(_dsa_kernel, S=S, Q=Q, topk=topk)
    full = lambda *shape: pl.BlockSpec(shape, lambda b, q: (0,) * len(shape))
    return pl.pallas_call(
        kern,
        grid=(B, S // Q),
        in_specs=[
            pl.BlockSpec((None, Q, MIX_A), lambda b, q: (b, q, AB_QA // MIX_A)),
            pl.BlockSpec((None, S, D_C), lambda b, q: (b, 0, AB_C // D_C)),
            pl.BlockSpec((None, Q, H_I * D_I), lambda b, q: (b, q, AB_QI // (H_I * D_I))),
            pl.BlockSpec((None, Q, LANES), lambda b, q: (b, q, AB_KW // LANES)),
            pl.BlockSpec((None, S, LANES), lambda b, q: (b, 0, AB_KW // LANES)),
            pl.BlockSpec((None, Q, MIX_A), lambda b, q: (b, q, AB_GATE // MIX_A)),
            full(1, D_C), full(D_C, DH_A), full(1, DH_A), full(1, DH_A), full(H_A, D_C, DH_A),
        ],
        out_specs=pl.BlockSpec((None, Q, MIX_A), lambda b, q: (b, q, 0)),
        out_shape=jax.ShapeDtypeStruct((B, S, MIX_A), BF16),
        scratch_shapes=[
            pltpu.VMEM((S // Q, D_C, Q), BF16),
            pltpu.VMEM((S, DH_A), BF16),
            pltpu.VMEM((S, D_I), BF16),
            pltpu.VMEM((S, Q), I32),
            pltpu.VMEM((S, Q), I16),
            pltpu.VMEM((S, Q), I16),
            pltpu.VMEM((S, Q), F32),
            pltpu.VMEM((S, H_A * Q), F32),
            pltpu.VMEM((1, Q), I32),
            pltpu.VMEM((H_A * Q, DH_A), BF16),
            pltpu.VMEM((D_C, H_A * Q), F32),
        ],
        compiler_params=pltpu.CompilerParams(
            dimension_semantics=("parallel", "arbitrary"), vmem_limit_bytes=VMEM_LIMIT),
        name="dsa_attention",
    )(proj, proj, proj, proj, proj, proj, kv_norm, w_uk, k_norm, q_norm, w_uv)


def _ret_kernel(lg_ref, q_ref, k_ref, v_ref, gate_ref, cos_ref, sin_ref, g_ref, o_ref, *, S, C):
    h = pl.program_id(1)
    lg = lg_ref[h]
    NT = (((1,), (1,)), ((), ()))
    TN = (((0,), (0,)), ((), ()))
    N = S // C
    ii = lax.broadcasted_iota(I32, (C, C), 0)
    jj = lax.broadcasted_iota(I32, (C, C), 1)
    diff = (ii - jj).astype(F32)
    dmask = jnp.where(diff >= 0, jnp.exp(jnp.maximum(diff, 0.0) * lg), 0.0)
    icol = lax.broadcasted_iota(I32, (C, 1), 0).astype(F32)
    k_dec = jnp.exp((C - 1 - icol) * lg)
    q_dec = jnp.exp((icol + 1.0) * lg)
    chunk_decay = jnp.exp(jnp.full((1, DV_B), float(C), F32) * lg)

    def rot(x, cos, sin):
        return x * cos + pltpu.roll(x, DK_B // 2, axis=1) * sin

    q, k, v = [], [], []
    for n in range(N):
        rows = slice(n * C, (n + 1) * C)
        cos, sin = cos_ref[rows, :], sin_ref[rows, :]
        q.append(rot(q_ref[rows, :], cos, sin))
        k.append(rot(k_ref[rows, :], cos, sin) * (DK_B ** -0.5))
        v.append(v_ref[rows, :])
    inner = [lax.dot_general(q[n], k[n], NT, preferred_element_type=F32) * dmask for n in range(N)]
    kv = [lax.dot_general(k[n] * k_dec, v[n], TN, preferred_element_type=F32) for n in range(N - 1)]
    out = [jnp.dot(inner[n], v[n], preferred_element_type=F32) for n in range(N)]
    st = kv[0]
    for n in range(1, N):
        out[n] = out[n] + jnp.dot(q[n] * q_dec, st, preferred_element_type=F32)
        if n + 1 < N:
            st = st * chunk_decay + kv[n]
    for n in range(N):
        rows = slice(n * C, (n + 1) * C)
        o = out[n] * lax.rsqrt(jnp.mean(out[n] * out[n], axis=-1, keepdims=True) + EPS) * g_ref[...]
        o_ref[rows, :] = (o * _silu(gate_ref[rows, :])).astype(o_ref.dtype)


def _retention(proj, ret_norm):
    B, S, _ = proj.shape
    C = RET_CHUNK
    pos = jnp.arange(S, dtype=F32)
    inv_freq = 1.0 / (RET_THETA ** jnp.linspace(0.0, 1.0, DK_B // 2, dtype=F32))
    ang = pos[:, None] * inv_freq[None, :]
    cos2 = jnp.concatenate([jnp.cos(ang), jnp.cos(ang)], axis=-1)
    sin2 = jnp.concatenate([-jnp.sin(ang), jnp.sin(ang)], axis=-1)
    log_gamma = jnp.log1p(-jnp.exp2(-5.0 - jnp.arange(H_B, dtype=F32)))
    kern = functools.partial(_ret_kernel, S=S, C=C)
    return pl.pallas_call(
        kern,
        grid_spec=pltpu.PrefetchScalarGridSpec(
            num_scalar_prefetch=1,
            grid=(B, H_B),
            in_specs=[
                pl.BlockSpec((None, S, DK_B), lambda b, h, lg: (b, 0, AB_QB // DK_B + h)),
                pl.BlockSpec((None, S, DK_B), lambda b, h, lg: (b, 0, AB_KB // DK_B + h)),
                pl.BlockSpec((None, S, DV_B), lambda b, h, lg: (b, 0, AB_VB // DV_B + h)),
                pl.BlockSpec((None, S, DV_B), lambda b, h, lg: (b, 0, (AB_GATE + MIX_A) // DV_B + h)),
                pl.BlockSpec((S, DK_B), lambda b, h, lg: (0, 0)),
                pl.BlockSpec((S, DK_B), lambda b, h, lg: (0, 0)),
                pl.BlockSpec((None, 1, DV_B), lambda b, h, lg: (h, 0, 0)),
            ],
            out_specs=pl.BlockSpec((None, S, DV_B), lambda b, h, lg: (b, 0, h)),
        ),
        out_shape=jax.ShapeDtypeStruct((B, S, MIX_B), BF16),
        compiler_params=pltpu.CompilerParams(
            dimension_semantics=("parallel", "arbitrary"), vmem_limit_bytes=VMEM_LIMIT),
        name="retention",
    )(log_gamma, proj, proj, proj, proj, cos2, sin2, ret_norm.reshape(H_B, 1, DV_B))


def _ab_out_kernel(a_ref, b_ref, x_ref, w_ref, o_ref):
    o_ref[...] = (x_ref[...] + jnp.dot(a_ref[...], w_ref[:MIX_A, :], preferred_element_type=F32)
                  + jnp.dot(b_ref[...], w_ref[MIX_A:, :], preferred_element_type=F32))


def _ab_out(a2d, b2d, x2d, w, tm=1024, tn=1024):
    T, D = x2d.shape
    return pl.pallas_call(
        _ab_out_kernel,
        grid=(T // tm, D // tn),
        in_specs=[
            pl.BlockSpec((tm, MIX_A), lambda i, j: (i, 0)),
            pl.BlockSpec((tm, MIX_B), lambda i, j: (i, 0)),
            pl.BlockSpec((tm, tn), lambda i, j: (i, j)),
            pl.BlockSpec((MIX_AB, tn), lambda i, j: (0, j)),
        ],
        out_specs=pl.BlockSpec((tm, tn), lambda i, j: (i, j)),
        out_shape=jax.ShapeDtypeStruct((T, D), F32),
        compiler_params=pltpu.CompilerParams(
            dimension_semantics=("parallel", "arbitrary"), vmem_limit_bytes=VMEM_LIMIT),
        name="ab_out_proj",
    )(a2d, b2d, x2d, w)


def _gdn_gates_kernel(ba_ref, alog_ref, dt_ref, beta_ref, gc_ref, *, S, C):
    ba = ba_ref[...].astype(F32)
    beta_ref[...] = _sigmoid(ba)
    xg = ba + dt_ref[...]
    softplus = jnp.maximum(xg, 0.0) + jnp.log(1.0 + jnp.exp(-jnp.abs(xg)))
    g_all = -jnp.exp(alog_ref[...]) * softplus
    rc = lax.broadcasted_iota(I32, (S, LANES), 0) & (C - 1)
    sh = 1
    while sh < C:
        g_all = g_all + jnp.where(rc >= sh, pltpu.roll(g_all, sh, axis=0), 0.0)
        sh *= 2
    gc_ref[...] = g_all


def _gdn_gates(proj, a_log, dt_bias):
    B, S, _ = proj.shape
    alog_row = jnp.zeros((1, LANES), F32).at[0, H_V:2 * H_V].set(a_log)
    dt_row = jnp.zeros((1, LANES), F32).at[0, H_V:2 * H_V].set(dt_bias)
    row = pl.BlockSpec((1, LANES), lambda b: (0, 0))
    out = pl.BlockSpec((None, S, LANES), lambda b: (b, 0, 0))
    return pl.pallas_call(
        functools.partial(_gdn_gates_kernel, S=S, C=DN_CHUNK),
        grid=(B,),
        in_specs=[pl.BlockSpec((None, S, LANES), lambda b: (b, 0, DN_BA // LANES)), row, row],
        out_specs=[out, out],
        out_shape=[jax.ShapeDtypeStruct((B, S, LANES), F32)] * 2,
        compiler_params=pltpu.CompilerParams(dimension_semantics=("parallel",), vmem_limit_bytes=VMEM_LIMIT),
        name="gdn_gates",
    )(proj, alog_row, dt_row)


def _gdn_kernel(q_ref, k_ref, v_ref, z_ref, beta_ref, gc_ref, cwq_ref, cwk_ref, cwv_ref, on_ref,
                o_ref, qn_s, kn_s, va_s, bb_s, gb_s, lhs_s, au_s, ku_s, gl_s, *, S, C, GC, U):
    g = pl.program_id(1)
    NG = S // (C * GC)
    REP = H_V // H_QK
    VW = REP * DV_C
    NT = (((1,), (1,)), ((), ()))
    TN = (((0,), (0,)), ((), ()))

    GR = GC * C
    HALO = 16

    def prep(gi):
        u, r0 = gi // NG, (gi % NG) * GR

        def conv_silu(ref, w, width):
            cols = slice(u * width, (u + 1) * width)
            w = w[:, cols]
            if r0 == 0:
                xw = jnp.concatenate([jnp.zeros((HALO, width), F32), ref[0:GR, cols].astype(F32)], axis=0)
            else:
                xw = ref[r0 - HALO:r0 + GR, cols].astype(F32)
            y = xw[HALO:] * w[CONV_K - 1:CONV_K, :]
            for d in range(1, CONV_K):
                y = y + pltpu.roll(xw, d, axis=0)[HALO:] * w[CONV_K - 1 - d:CONV_K - d, :]
            return _silu(y)

        def l2n(x):
            return x * lax.rsqrt(jnp.sum(x * x, axis=-1, keepdims=True) + EPS)

        qn_s[u, r0:r0 + GR, :] = l2n(conv_silu(q_ref, cwq_ref[...], DK_C)) * (DK_C ** -0.5)
        kn_s[u, r0:r0 + GR, :] = l2n(conv_silu(k_ref, cwk_ref[...], DK_C))
        va_s[u, r0:r0 + GR, :] = conv_silu(v_ref, cwv_ref[...], VW)
        lane = lax.broadcasted_iota(I32, (GR, LANES), 1)
        for j in range(REP):
            hv = (g * U + u) * REP + j
            bcol = jnp.sum(jnp.where(lane == hv, beta_ref[r0:r0 + GR, :], 0.0), axis=-1, keepdims=True)
            gcol = jnp.sum(jnp.where(lane == hv + H_V, gc_ref[r0:r0 + GR, :], 0.0), axis=-1, keepdims=True)
            bb_s[u, j, r0:r0 + GR, :] = jnp.broadcast_to(bcol, (GR, LANES))
            gb_s[u, j, r0:r0 + GR, :] = jnp.broadcast_to(gcol, (GR, LANES))

    assert REP * C == LANES
    ii = lax.broadcasted_iota(I32, (REP * C, REP * C), 0)
    jj = lax.broadcasted_iota(I32, (REP * C, REP * C), 1)
    same_head = (ii >= C) == (jj >= C)
    tri = same_head & (ii >= jj)
    strict = same_head & (ii > jj)
    eye = jnp.where(ii == jj, 1.0, 0.0)

    dot = functools.partial(jnp.dot, preferred_element_type=F32)
    chains = [(c, j) for c in range(GC) for j in range(REP)]

    def row0(gi, c):
        return ((gi % NG) * GC + c) * C

    def scan_step(gi, c, states):
        u, r0 = gi // NG, row0(gi, c)
        for j in range(REP):
            i = c * REP + j
            cols = slice(u * VW + j * DV_C, u * VW + (j + 1) * DV_C)
            r = dot(lhs_s[i], states[j].astype(BF16))
            o = r[DK_C:] + au_s[i]
            states[j] = states[j] * gl_s[i, 0:1, :] + ku_s[i] - r[:DK_C]
            o = o * lax.rsqrt(jnp.mean(o * o, axis=-1, keepdims=True) + EPS) * on_ref[...]
            o = o * _silu(z_ref[r0:r0 + C, cols].astype(F32))
            o_ref[r0:r0 + C, cols] = o.astype(o_ref.dtype)

    def stage(gi, pending):
        pending = list(pending)

        def tick():
            if pending:
                pending.pop(0)()
        u = gi // NG
        r0s = [row0(gi, c) for c in range(GC)]
        qc = [qn_s[u, r0s[c]:r0s[c] + C, :] for c in range(GC)]
        kc = [kn_s[u, r0s[c]:r0s[c] + C, :] for c in range(GC)]
        qkk = [lax.dot_general(jnp.concatenate([qc[c], kc[c]], axis=0).astype(BF16),
                               jnp.concatenate([kc[c], kc[c]], axis=0).astype(BF16),
                               NT, preferred_element_type=F32) for c in range(GC)]
        tick()
        gcb, bb, p, attn = {}, {}, [], []
        for c in range(GC):
            for j in range(REP):
                gcb[c, j] = gb_s[u, j, r0s[c]:r0s[c] + C, :]
                bb[c, j] = bb_s[u, j, r0s[c]:r0s[c] + C, :]
            g2 = jnp.concatenate([gcb[c, j] for j in range(REP)], axis=0)
            b2 = jnp.concatenate([bb[c, j] for j in range(REP)], axis=0)
            gcr = jnp.sum(jnp.where(ii == jj, g2, 0.0), axis=0, keepdims=True)
            decay = jnp.exp(jnp.where(tri, g2 - gcr, -jnp.inf))
            kk2 = jnp.concatenate([qkk[c][C:]] * REP, axis=0)
            qk2 = jnp.concatenate([qkk[c][:C]] * REP, axis=0)
            p.append(jnp.where(strict, -(kk2 * b2 * decay), 0.0))
            attn.append(qk2 * decay)
        qm = [dot(p[c].astype(BF16), p[c].astype(BF16)) for c in range(GC)]
        tick()
        sm = [eye + p[c] for c in range(GC)]
        for _ in range(4):
            r = [dot(qm[c].astype(BF16), jnp.concatenate([qm[c], sm[c]], axis=1).astype(BF16)) for c in range(GC)]
            tick()
            qm = [r[c][:, :REP * C] for c in range(GC)]
            sm = [sm[c] + r[c][:, REP * C:] for c in range(GC)]
        tinv = [sm[c] + dot(qm[c].astype(BF16), sm[c].astype(BF16)) for c in range(GC)]
        tick()
        uw2, glast = [], {}
        for c in range(GC):
            rhs = []
            for j in range(REP):
                vb = va_s[u, r0s[c]:r0s[c] + C, j * DV_C:(j + 1) * DV_C] * bb[c, j]
                kbg = kc[c] * bb[c, j] * jnp.exp(gcb[c, j])
                rhs.append(jnp.concatenate([vb, kbg], axis=1))
                glast[c, j] = gcb[c, j][C - 1:C, :]
            uw2.append(dot(tinv[c].astype(BF16), jnp.concatenate(rhs, axis=0).astype(BF16)))
        tick()
        uw = {(c, j): uw2[c][j * C:(j + 1) * C] for c, j in chains}
        ktuw = {(c, j): lax.dot_general(kc[c] * jnp.exp(glast[c, j] - gcb[c, j]), uw[c, j], TN,
                                        preferred_element_type=F32) for c, j in chains}
        tick()
        atuw2 = [dot(attn[c].astype(BF16), uw2[c].astype(BF16)) for c in range(GC)]
        atuw = {(c, j): atuw2[c][j * C:(j + 1) * C] for c, j in chains}
        while pending:
            tick()
        for c, j in chains:
            i = c * REP + j
            lhs_s[i] = jnp.concatenate([ktuw[c, j][:, DV_C:], qc[c] * jnp.exp(gcb[c, j]) - atuw[c, j][:, DV_C:]],
                                       axis=0).astype(BF16)
            au_s[i] = atuw[c, j][:, :DV_C]
            ku_s[i] = ktuw[c, j][:, :DV_C]
            gl_s[i] = jnp.broadcast_to(jnp.exp(glast[c, j]), (8, LANES))

    states = [[jnp.zeros((DK_C, DV_C), F32) for _ in range(REP)] for _ in range(U)]
    prep(0)
    for gi in range(U * NG):
        if gi + 1 < U * NG:
            prep(gi + 1)
        stage(gi, [functools.partial(scan_step, gi - 1, c, states[(gi - 1) // NG]) for c in range(GC)] if gi else [])
    for c in range(GC):
        scan_step(U * NG - 1, c, states[U - 1])


def _gdn(proj, conv_w, a_log, dt_bias, out_norm):
    B, S, _ = proj.shape
    C = DN_CHUNK
    REP = H_V // H_QK
    VW = REP * DV_C
    GC = 8
    U = 2
    assert S % (C * GC) == 0 and H_QK % U == 0
    beta_all, gc_all = _gdn_gates(proj, a_log, dt_bias)
    kern = functools.partial(_gdn_kernel, S=S, C=C, GC=GC, U=U)
    KW, UW = U * DK_C, U * VW
    return pl.pallas_call(
        kern,
        grid=(B, H_QK // U),
        in_specs=[
            pl.BlockSpec((None, S, KW), lambda b, g: (b, 0, DN_Q // KW + g)),
            pl.BlockSpec((None, S, KW), lambda b, g: (b, 0, DN_K // KW + g)),
            pl.BlockSpec((None, S, UW), lambda b, g: (b, 0, DN_V // UW + g)),
            pl.BlockSpec((None, S, UW), lambda b, g: (b, 0, DN_Z // UW + g)),
            pl.BlockSpec((None, S, LANES), lambda b, g: (b, 0, 0)),
            pl.BlockSpec((None, S, LANES), lambda b, g: (b, 0, 0)),
            pl.BlockSpec((CONV_K, KW), lambda b, g: (0, DN_Q // KW + g)),
            pl.BlockSpec((CONV_K, KW), lambda b, g: (0, DN_K // KW + g)),
            pl.BlockSpec((CONV_K, UW), lambda b, g: (0, DN_V // UW + g)),
            pl.BlockSpec((1, DV_C), lambda b, g: (0, 0)),
        ],
        out_specs=pl.BlockSpec((None, S, UW), lambda b, g: (b, 0, g)),
        out_shape=jax.ShapeDtypeStruct((B, S, MIX_C), BF16),
        scratch_shapes=[
            pltpu.VMEM((U, S, DK_C), F32),
            pltpu.VMEM((U, S, DK_C), F32),
            pltpu.VMEM((U, S, VW), F32),
            pltpu.VMEM((U, REP, S, LANES), F32),
            pltpu.VMEM((U, REP, S, LANES), F32),
            pltpu.VMEM((GC * REP, DK_C + C, DK_C), BF16),
            pltpu.VMEM((GC * REP, C, DV_C), F32),
            pltpu.VMEM((GC * REP, DK_C, DV_C), F32),
            pltpu.VMEM((GC * REP, 8, LANES), F32),
        ],
        compiler_params=pltpu.CompilerParams(
            dimension_semantics=("parallel", "arbitrary"), vmem_limit_bytes=VMEM_LIMIT),
        name="gated_deltanet",
    )(proj, proj, proj, proj, beta_all, gc_all, conv_w, conv_w, conv_w, out_norm.reshape(1, DV_C))


def _dn_out_kernel(o_in_ref, x_ref, w_ref, o_ref):
    o_ref[...] = x_ref[...] + jnp.dot(o_in_ref[...], w_ref[...], preferred_element_type=F32)


def _dn_out(o2d, x2d, w, tm=1024, tn=1024):
    T, D = x2d.shape
    K = o2d.shape[1]
    return pl.pallas_call(
        _dn_out_kernel,
        grid=(T // tm, D // tn),
        in_specs=[
            pl.BlockSpec((tm, K), lambda i, j: (i, 0)),
            pl.BlockSpec((tm, tn), lambda i, j: (i, j)),
            pl.BlockSpec((K, tn), lambda i, j: (0, j)),
        ],
        out_specs=pl.BlockSpec((tm, tn), lambda i, j: (i, j)),
        out_shape=jax.ShapeDtypeStruct((T, D), F32),
        compiler_params=pltpu.CompilerParams(
            dimension_semantics=("parallel", "arbitrary"), vmem_limit_bytes=VMEM_LIMIT),
        name="dn_out_proj",
    )(o2d, x2d, w)


def _ab_weight(w):
    D = w.shape[0]
    o = np.cumsum([0, MIX_A, D_C, H_I * D_I, D_I, H_I, H_B * DK_B, H_B * DK_B, MIX_B, MIX_AB])
    q_a, c, q_i, k_i, w_i, q_b, k_b, v_b, gate = [w[:, o[i]:o[i + 1]].astype(BF16) for i in range(9)]
    pad_kw = jnp.zeros((D, LANES - D_I - H_I), BF16)
    pad_end = jnp.zeros((D, AB_NP - AB_KW - LANES), BF16)
    return jnp.concatenate([q_a, v_b, gate, q_i, q_b, k_b, c, k_i, w_i, pad_kw, pad_end], axis=1)


def _cast_tiles_kernel(w_ref, tail_ref, o_ref, *, n_main):
    @pl.when(pl.program_id(0) < n_main)
    def _():
        o_ref[...] = w_ref[...].astype(BF16)

    @pl.when(pl.program_id(0) >= n_main)
    def _():
        o_ref[...] = tail_ref[...].astype(BF16)


def _dn_weight(w, tw=512):
    D, N = w.shape
    n_main = N // tw
    assert DN_NP - n_main * tw == tw
    tail = jnp.pad(w[:, n_main * tw:], ((0, 0), (0, DN_NP - N)))
    return pl.pallas_call(
        functools.partial(_cast_tiles_kernel, n_main=n_main),
        grid=(DN_NP // tw,),
        in_specs=[pl.BlockSpec((D, tw), lambda j: (0, jnp.minimum(j, n_main - 1))),
                  pl.BlockSpec((D, tw), lambda j: (0, 0))],
        out_specs=pl.BlockSpec((D, tw), lambda j: (0, j)),
        out_shape=jax.ShapeDtypeStruct((D, DN_NP), BF16),
        compiler_params=pltpu.CompilerParams(dimension_semantics=("parallel",), vmem_limit_bytes=VMEM_LIMIT),
        name="dn_weight_cast",
    )(w, tail)


def kernel(x, ab_norm, ab_w_in, ab_kv_norm, ab_w_uk, ab_w_uv, ab_q_norm, ab_k_norm, ab_ret_norm, ab_w_out,
           dn_norm, dn_w_in, dn_conv, dn_a_log, dn_dt_bias, dn_out_norm, dn_w_out):
    B, S, D = x.shape
    T = B * S
    depth = ab_norm.shape[0] + dn_norm.shape[0]
    h = x.astype(F32).reshape(T, D)
    for layer in range(depth):
        j = layer // 2
        if layer % 2 == 0:
            proj = _norm_proj(h, ab_norm[j].reshape(1, D), _ab_weight(ab_w_in[j]), tm=1024, tn=1536)
            proj3 = proj.reshape(B, S, AB_NP)
            a_out = _dsa(proj3, ab_kv_norm[j].reshape(1, D_C), ab_w_uk[j].astype(BF16),
                         ab_k_norm[j].reshape(1, DH_A), ab_q_norm[j].reshape(1, DH_A), ab_w_uv[j].astype(BF16))
            b_out = _retention(proj3, ab_ret_norm[j])
            h = _ab_out(a_out.reshape(T, MIX_A), b_out.reshape(T, MIX_B), h, ab_w_out[j].astype(BF16))
        else:
            proj = _norm_proj(h, dn_norm[j].reshape(1, D), _dn_weight(dn_w_in[j]), tm=1024, tn=1280, out_dtype=BF16)
            o = _gdn(proj.reshape(B, S, DN_NP), dn_conv[j], dn_a_log[j], dn_dt_bias[j], dn_out_norm[j])
            h = _dn_out(o.reshape(T, MIX_C), h, dn_w_out[j].astype(BF16))
    return h.reshape(B, S, D).astype(x.dtype)
```

```python
import functools

import numpy as np
import jax
import jax.numpy as jnp
from jax import lax
from jax.experimental import pallas as pl
from jax.experimental.pallas import tpu as pltpu

F32 = jnp.float32
BF16 = jnp.bfloat16
I32 = jnp.int32
I16 = jnp.int16

EPS = 1e-6
H_A, DH_A, D_C, H_I, D_I = 8, 128, 256, 8, 64
INDEX_TOPK = 256
H_B, DK_B, DV_B, RET_CHUNK, RET_THETA = 4, 128, 256, 128, 10000.0
H_QK, H_V, DK_C, DV_C, CONV_K, DN_CHUNK = 16, 32, 128, 128, 4, 64

MIX_A = H_A * DH_A
MIX_B = H_B * DV_B
MIX_AB = MIX_A + MIX_B
MIX_C = H_V * DV_C

LANES = 128
INT_MIN = np.int32(-2**31)

AB_QA, AB_VB, AB_GATE, AB_QI, AB_QB, AB_KB, AB_C, AB_KW, AB_NP = 0, 1024, 2048, 4096, 4608, 5120, 5632, 5888, 6144
DN_Q, DN_K, DN_V, DN_Z, DN_BA, DN_NP = 0, 2048, 4096, 8192, 12288, 12800

VMEM_LIMIT = 56 * 1024 * 1024


def _sigmoid(x):
    return 1.0 / (1.0 + jnp.exp(-x))


def _silu(x):
    return x * (0.5 * jnp.tanh(0.5 * x) + 0.5)


def _norm_proj_kernel(x_ref, g_ref, w_ref, o_ref, xn_ref):
    @pl.when(pl.program_id(1) == 0)
    def _():
        x = x_ref[...]
        ms = jnp.mean(x * x, axis=-1, keepdims=True)
        xn_ref[...] = (x * lax.rsqrt(ms + EPS) * g_ref[...]).astype(BF16)

    o_ref[...] = jnp.dot(xn_ref[...], w_ref[...], preferred_element_type=F32).astype(o_ref.dtype)


def _norm_proj(x2d, gain, w, tm, tn, out_dtype=F32):
    T, D = x2d.shape
    N = w.shape[1]
    return pl.pallas_call(
        _norm_proj_kernel,
        grid=(T // tm, N // tn),
        in_specs=[
            pl.BlockSpec((tm, D), lambda i, j: (i, 0)),
            pl.BlockSpec((1, D), lambda i, j: (0, 0)),
            pl.BlockSpec((D, tn), lambda i, j: (0, j)),
        ],
        out_specs=pl.BlockSpec((tm, tn), lambda i, j: (i, j)),
        out_shape=jax.ShapeDtypeStruct((T, N), out_dtype),
        scratch_shapes=[pltpu.VMEM((tm, D), BF16)],
        compiler_params=pltpu.CompilerParams(
            dimension_semantics=("parallel", "arbitrary"), vmem_limit_bytes=VMEM_LIMIT),
        name="norm_proj",
    )(x2d, gain, w)


def _dsa_kernel(qa_ref, c_ref, qi_ref, kwq_ref, kwk_ref, gate_ref, kvn_ref, wuk_ref, kn_ref, qn_ref, wuv_ref,
                o_ref, cnT_s, k_s, ki_s, key_s, khi_s, klo_s, bias_s, lg_s, j_s, qall_s, lat_s, *, S, Q, topk):
    qb = pl.program_id(1)
    nkb = qb + 1
    RB = 128
    nrb = nkb * (Q // RB)
    NT = (((1,), (1,)), ((), ()))

    @pl.when(qb == 0)
    def _prep():
        def body(j, _):
            r0 = pl.multiple_of(j * Q, Q)
            c = c_ref[pl.ds(r0, Q), :]
            cn = c * lax.rsqrt(jnp.mean(c * c, axis=-1, keepdims=True) + EPS) * kvn_ref[...]
            kk = jnp.dot(cn.astype(BF16), wuk_ref[...], preferred_element_type=F32)
            kk = kk * lax.rsqrt(jnp.mean(kk * kk, axis=-1, keepdims=True) + EPS) * kn_ref[...]
            cnT_s[j] = cn.T.astype(BF16)
            k_s[pl.ds(r0, Q), :] = kk.astype(BF16)
            ki_s[pl.ds(r0, Q), :] = kwk_ref[pl.ds(r0, Q), 0:D_I].astype(BF16)
            return 0
        lax.fori_loop(0, S // Q, body, 0)

    tpos = qb * Q + lax.broadcasted_iota(I32, (1, Q), 1)
    w_t = kwq_ref[...].T[D_I:D_I + H_I, :] * (H_I ** -0.5 * D_I ** -0.5)

    qi_heads = [qi_ref[:, h * D_I:(h + 1) * D_I].astype(BF16) for h in range(H_I)]

    def score_blk(j, _):
        r0 = pl.multiple_of(j * Q, Q)
        kib = ki_s[pl.ds(r0, Q), :]
        acc = jnp.zeros((Q, Q), F32)
        for h in range(H_I):
            s = lax.dot_general(kib, qi_heads[h], NT, preferred_element_type=F32)
            acc = acc + jnp.maximum(s, 0.0) * w_t[h:h + 1, :]
        acc = acc + 0.0
        bits = pltpu.bitcast(acc, I32)
        key = bits ^ ((bits >> 31) & np.int32(0x7FFFFFFF))
        kpos = r0 + lax.broadcasted_iota(I32, (Q, Q), 0)
        key = jnp.where(kpos <= tpos, key, INT_MIN)
        key_s[pl.ds(r0, Q), :] = key
        khi_s[pl.ds(r0, Q), :] = (key >> 16).astype(I16)
        return 0
    lax.fori_loop(0, nkb, score_blk, 0)

    def count(pred):
        def body(i, acc):
            r0 = pl.multiple_of(i * RB, RB)
            kpos = r0 + lax.broadcasted_iota(I32, (RB, Q), 0)
            return acc + jnp.where(pred(key_s[pl.ds(r0, RB), :], kpos), 1.0, 0.0)
        acc = lax.fori_loop(0, nrb, body, jnp.zeros((RB, Q), F32))
        return jnp.sum(acc, axis=0, keepdims=True)

    def count16(ref, pred):
        def body(j, acc):
            r0 = pl.multiple_of(j * Q, Q)
            return acc + jnp.where(pred(ref[pl.ds(r0, Q), :]), np.int16(1), np.int16(0))
        acc = lax.fori_loop(0, nkb, body, jnp.zeros((Q, Q), I16))
        acc = jnp.sum(acc.reshape(Q // 16, 16, Q), axis=0)
        return jnp.sum(acc.astype(F32), axis=0, keepdims=True)

    def search16(ref, base):
        def bit_body(i, prefix_u):
            cand_u = prefix_u | jnp.left_shift(np.int32(1), 15 - i)
            cand_s = (cand_u - HALF16).astype(I16)
            cnt = base + count16(ref, lambda k: k >= cand_s)
            return jnp.where(cnt >= float(topk), cand_u, prefix_u)
        return lax.fori_loop(0, 16, bit_body, jnp.zeros((1, Q), I32))

    HALF16 = np.int32(1 << 15)
    th = search16(khi_s, 0.0) - HALF16
    th16 = th.astype(I16)
    cnt_hi_gt = count16(khi_s, lambda k: k > th16)

    def lo_blk(j, _):
        r0 = pl.multiple_of(j * Q, Q)
        k = key_s[pl.ds(r0, Q), :]
        lo = (k & np.int32(0xFFFF)) - HALF16
        klo_s[pl.ds(r0, Q), :] = jnp.where((k >> 16) == th, lo, -HALF16).astype(I16)
        return 0
    lax.fori_loop(0, nkb, lo_blk, 0)
    tl_u = search16(klo_s, cnt_hi_gt)
    thr = (th << 16) | tl_u

    tl16 = (tl_u - HALF16).astype(I16)
    cnt_ge = cnt_hi_gt + count16(klo_s, lambda k: k >= tl16)
    ambiguous = (cnt_ge != float(topk)) & (tpos >= topk - 1)
    j_s[...] = jnp.full((1, Q), S, I32)

    @pl.when(jnp.max(jnp.where(ambiguous, 1.0, 0.0)) > 0.0)
    def _ties():
        need = float(topk) - count(lambda k, _: k > thr)
        nbits = int(S).bit_length() - 1

        def jbit(i, prefix):
            cand = prefix | jnp.left_shift(np.int32(1), nbits - 1 - i)
            cnt = count(lambda k, kpos: (k == thr) & (kpos < cand))
            return jnp.where(cnt < need, cand, prefix)
        jstar = lax.fori_loop(0, nbits, jbit, jnp.zeros((1, Q), I32))
        j_s[...] = jnp.where(ambiguous, jstar, S)

    jsel = j_s[...]

    def bias_blk(i, _):
        r0 = pl.multiple_of(i * RB, RB)
        k = key_s[pl.ds(r0, RB), :]
        kpos = r0 + lax.broadcasted_iota(I32, (RB, Q), 0)
        sel = (kpos <= tpos) & ((k > thr) | ((k == thr) & (kpos <= jsel)))
        bias_s[pl.ds(r0, RB), :] = jnp.where(sel, 0.0, -jnp.inf)
        return 0
    lax.fori_loop(0, nrb, bias_blk, 0)

    for h in range(H_A):
        qh = qa_ref[:, h * DH_A:(h + 1) * DH_A]
        qh = qh * lax.rsqrt(jnp.mean(qh * qh, axis=-1, keepdims=True) + EPS) * qn_ref[...] * (DH_A ** -0.5)
        qall_s[h * Q:(h + 1) * Q, :] = qh.astype(BF16)

    def pass_a(j, m):
        r0 = pl.multiple_of(j * Q, Q)
        s = lax.dot_general(k_s[pl.ds(r0, Q), :], qall_s[...], NT, preferred_element_type=F32)
        bias = bias_s[pl.ds(r0, Q), :]
        ms = []
        for h in range(H_A):
            sh = s[:, h * Q:(h + 1) * Q] + bias
            lg_s[pl.ds(r0, Q), h * Q:(h + 1) * Q] = sh
            ms.append(jnp.max(sh, axis=0, keepdims=True))
        return jnp.maximum(m, jnp.concatenate(ms, axis=1))
    m = lax.fori_loop(0, nkb, pass_a, jnp.full((1, H_A * Q), -jnp.inf, F32))

    lat_s[...] = jnp.zeros_like(lat_s)

    def pass_b(j, l):
        r0 = pl.multiple_of(j * Q, Q)
        p = jnp.exp(lg_s[pl.ds(r0, Q), :] - m)
        lat_s[...] += jnp.dot(cnT_s[j], p.astype(BF16), preferred_element_type=F32)
        return l + jnp.sum(p, axis=0, keepdims=True)
    l = lax.fori_loop(0, nkb, pass_b, jnp.zeros((1, H_A * Q), F32))

    lat = (lat_s[...] * (1.0 / l)).astype(BF16)
    for h in range(H_A):
        cols = slice(h * DH_A, (h + 1) * DH_A)
        out_h = lax.dot_general(lat[:, h * Q:(h + 1) * Q], wuv_ref[h], (((0,), (0,)), ((), ())),
                                preferred_element_type=F32)
        o_ref[:, cols] = (out_h * _silu(gate_ref[:, cols])).astype(o_ref.dtype)


def _dsa(proj, kv_norm, w_uk, k_norm, q_norm, w_uv, Q=256):
    B, S, _ = proj.shape
    topk = min(INDEX_TOPK, S // 4)
    assert S % Q == 0 and topk <= Q and (S & (S - 1)) == 0
    kern = functools.partial(_dsa_kernel, S=S, Q=Q, topk=topk)
    full = lambda *shape: pl.BlockSpec(shape, lambda b, q: (0,) * len(shape))
    return pl.pallas_call(
        kern,
        grid=(B, S // Q),
        in_specs=[
            pl.BlockSpec((None, Q, MIX_A), lambda b, q: (b, q, AB_QA // MIX_A)),
            pl.BlockSpec((None, S, D_C), lambda b, q: (b, 0, AB_C // D_C)),
            pl.BlockSpec((None, Q, H_I * D_I), lambda b, q: (b, q, AB_QI // (H_I * D_I))),
            pl.BlockSpec((None, Q, LANES), lambda b, q: (b, q, AB_KW // LANES)),
            pl.BlockSpec((None, S, LANES), lambda b, q: (b, 0, AB_KW // LANES)),
            pl.BlockSpec((None, Q, MIX_A), lambda b, q: (b, q, AB_GATE // MIX_A)),
            full(1, D_C), full(D_C, DH_A), full(1, DH_A), full(1, DH_A), full(H_A, D_C, DH_A),
        ],
        out_specs=pl.BlockSpec((None, Q, MIX_A), lambda b, q: (b, q, 0)),
        out_shape=jax.ShapeDtypeStruct((B, S, MIX_A), BF16),
        scratch_shapes=[
            pltpu.VMEM((S // Q, D_C, Q), BF16),
            pltpu.VMEM((S, DH_A), BF16),
            pltpu.VMEM((S, D_I), BF16),
            pltpu.VMEM((S, Q), I32),
            pltpu.VMEM((S, Q), I16),
            pltpu.VMEM((S, Q), I16),
            pltpu.VMEM((S, Q), F32),
            pltpu.VMEM((S, H_A * Q), F32),
            pltpu.VMEM((1, Q), I32),
            pltpu.VMEM((H_A * Q, DH_A), BF16),
            pltpu.VMEM((D_C, H_A * Q), F32),
        ],
        compiler_params=pltpu.CompilerParams(
            dimension_semantics=("parallel", "arbitrary"), vmem_limit_bytes=VMEM_LIMIT),
        name="dsa_attention",
    )(proj, proj, proj, proj, proj, proj, kv_norm, w_uk, k_norm, q_norm, w_uv)


def _ret_kernel(lg_ref, q_ref, k_ref, v_ref, gate_ref, cos_ref, sin_ref, g_ref, o_ref, *, S, C):
    h = pl.program_id(1)
    lg = lg_ref[h]
    NT = (((1,), (1,)), ((), ()))
    TN = (((0,), (0,)), ((), ()))
    N = S // C
    ii = lax.broadcasted_iota(I32, (C, C), 0)
    jj = lax.broadcasted_iota(I32, (C, C), 1)
    diff = (ii - jj).astype(F32)
    dmask = jnp.where(diff >= 0, jnp.exp(jnp.maximum(diff, 0.0) * lg), 0.0)
    icol = lax.broadcasted_iota(I32, (C, 1), 0).astype(F32)
    k_dec = jnp.exp((C - 1 - icol) * lg)
    q_dec = jnp.exp((icol + 1.0) * lg)
    chunk_decay = jnp.exp(jnp.full((1, DV_B), float(C), F32) * lg)

    def rot(x, cos, sin):
        return x * cos + pltpu.roll(x, DK_B // 2, axis=1) * sin

    q, k, v = [], [], []
    for n in range(N):
        rows = slice(n * C, (n + 1) * C)
        cos, sin = cos_ref[rows, :], sin_ref[rows, :]
        q.append(rot(q_ref[rows, :], cos, sin))
        k.append(rot(k_ref[rows, :], cos, sin) * (DK_B ** -0.5))
        v.append(v_ref[rows, :])
    inner = [lax.dot_general(q[n], k[n], NT, preferred_element_type=F32) * dmask for n in range(N)]
    kv = [lax.dot_general(k[n] * k_dec, v[n], TN, preferred_element_type=F32) for n in range(N - 1)]
    out = [jnp.dot(inner[n], v[n], preferred_element_type=F32) for n in range(N)]
    st = kv[0]
    for n in range(1, N):
        out[n] = out[n] + jnp.dot(q[n] * q_dec, st, preferred_element_type=F32)
        if n + 1 < N:
            st = st * chunk_decay + kv[n]
    for n in range(N):
        rows = slice(n * C, (n + 1) * C)
        o = out[n] * lax.rsqrt(jnp.mean(out[n] * out[n], axis=-1, keepdims=True) + EPS) * g_ref[...]
        o_ref[rows, :] = (o * _silu(gate_ref[rows, :])).astype(o_ref.dtype)


def _retention(proj, ret_norm):
    B, S, _ = proj.shape
    C = RET_CHUNK
    pos = jnp.arange(S, dtype=F32)
    inv_freq = 1.0 / (RET_THETA ** jnp.linspace(0.0, 1.0, DK_B // 2, dtype=F32))
    ang = pos[:, None] * inv_freq[None, :]
    cos2 = jnp.concatenate([jnp.cos(ang), jnp.cos(ang)], axis=-1)
    sin2 = jnp.concatenate([-jnp.sin(ang), jnp.sin(ang)], axis=-1)
    log_gamma = jnp.log1p(-jnp.exp2(-5.0 - jnp.arange(H_B, dtype=F32)))
    kern = functools.partial(_ret_kernel, S=S, C=C)
    return pl.pallas_call(
        kern,
        grid_spec=pltpu.PrefetchScalarGridSpec(
            num_scalar_prefetch=1,
            grid=(B, H_B),
            in_specs=[
                pl.BlockSpec((None, S, DK_B), lambda b, h, lg: (b, 0, AB_QB // DK_B + h)),
                pl.BlockSpec((None, S, DK_B), lambda b, h, lg: (b, 0, AB_KB // DK_B + h)),
                pl.BlockSpec((None, S, DV_B), lambda b, h, lg: (b, 0, AB_VB // DV_B + h)),
                pl.BlockSpec((None, S, DV_B), lambda b, h, lg: (b, 0, (AB_GATE + MIX_A) // DV_B + h)),
                pl.BlockSpec((S, DK_B), lambda b, h, lg: (0, 0)),
                pl.BlockSpec((S, DK_B), lambda b, h, lg: (0, 0)),
                pl.BlockSpec((None, 1, DV_B), lambda b, h, lg: (h, 0, 0)),
            ],
            out_specs=pl.BlockSpec((None, S, DV_B), lambda b, h, lg: (b, 0, h)),
        ),
        out_shape=jax.ShapeDtypeStruct((B, S, MIX_B), BF16),
        compiler_params=pltpu.CompilerParams(
            dimension_semantics=("parallel", "arbitrary"), vmem_limit_bytes=VMEM_LIMIT),
        name="retention",
    )(log_gamma, proj, proj, proj, proj, cos2, sin2, ret_norm.reshape(H_B, 1, DV_B))


def _ab_out_kernel(a_ref, b_ref, x_ref, w_ref, o_ref):
    o_ref[...] = (x_ref[...] + jnp.dot(a_ref[...], w_ref[:MIX_A, :], preferred_element_type=F32)
                  + jnp.dot(b_ref[...], w_ref[MIX_A:, :], preferred_element_type=F32))


def _ab_out(a2d, b2d, x2d, w, tm=1024, tn=1024):
    T, D = x2d.shape
    return pl.pallas_call(
        _ab_out_kernel,
        grid=(T // tm, D // tn),
        in_specs=[
            pl.BlockSpec((tm, MIX_A), lambda i, j: (i, 0)),
            pl.BlockSpec((tm, MIX_B), lambda i, j: (i, 0)),
            pl.BlockSpec((tm, tn), lambda i, j: (i, j)),
            pl.BlockSpec((MIX_AB, tn), lambda i, j: (0, j)),
        ],
        out_specs=pl.BlockSpec((tm, tn), lambda i, j: (i, j)),
        out_shape=jax.ShapeDtypeStruct((T, D), F32),
        compiler_params=pltpu.CompilerParams(
            dimension_semantics=("parallel", "arbitrary"), vmem_limit_bytes=VMEM_LIMIT),
        name="ab_out_proj",
    )(a2d, b2d, x2d, w)


def _gdn_gates_kernel(ba_ref, alog_ref, dt_ref, beta_ref, gc_ref, *, S, C):
    ba = ba_ref[...].astype(F32)
    beta_ref[...] = _sigmoid(ba)
    xg = ba + dt_ref[...]
    softplus = jnp.maximum(xg, 0.0) + jnp.log(1.0 + jnp.exp(-jnp.abs(xg)))
    g_all = -jnp.exp(alog_ref[...]) * softplus
    rc = lax.broadcasted_iota(I32, (S, LANES), 0) & (C - 1)
    sh = 1
    while sh < C:
        g_all = g_all + jnp.where(rc >= sh, pltpu.roll(g_all, sh, axis=0), 0.0)
        sh *= 2
    gc_ref[...] = g_all


def _gdn_gates(proj, a_log, dt_bias):
    B, S, _ = proj.shape
    alog_row = jnp.zeros((1, LANES), F32).at[0, H_V:2 * H_V].set(a_log)
    dt_row = jnp.zeros((1, LANES), F32).at[0, H_V:2 * H_V].set(dt_bias)
    row = pl.BlockSpec((1, LANES), lambda b: (0, 0))
    out = pl.BlockSpec((None, S, LANES), lambda b: (b, 0, 0))
    return pl.pallas_call(
        functools.partial(_gdn_gates_kernel, S=S, C=DN_CHUNK),
        grid=(B,),
        in_specs=[pl.BlockSpec((None, S, LANES), lambda b: (b, 0, DN_BA // LANES)), row, row],
        out_specs=[out, out],
        out_shape=[jax.ShapeDtypeStruct((B, S, LANES), F32)] * 2,
        compiler_params=pltpu.CompilerParams(dimension_semantics=("parallel",), vmem_limit_bytes=VMEM_LIMIT),
        name="gdn_gates",
    )(proj, alog_row, dt_row)


def _gdn_kernel(q_ref, k_ref, v_ref, z_ref, beta_ref, gc_ref, cwq_ref, cwk_ref, cwv_ref, on_ref,
                o_ref, qn_s, kn_s, va_s, bb_s, gb_s, lhs_s, au_s, ku_s, gl_s, *, S, C, GC, U):
    g = pl.program_id(1)
    NG = S // (C * GC)
    REP = H_V // H_QK
    VW = REP * DV_C
    NT = (((1,), (1,)), ((), ()))
    TN = (((0,), (0,)), ((), ()))

    GR = GC * C
    HALO = 16

    def prep(gi):
        u, r0 = gi // NG, (gi % NG) * GR

        def conv_silu(ref, w, width):
            cols = slice(u * width, (u + 1) * width)
            w = w[:, cols]
            if r0 == 0:
                xw = jnp.concatenate([jnp.zeros((HALO, width), F32), ref[0:GR, cols].astype(F32)], axis=0)
            else:
                xw = ref[r0 - HALO:r0 + GR, cols].astype(F32)
            y = xw[HALO:] * w[CONV_K - 1:CONV_K, :]
            for d in range(1, CONV_K):
                y = y + pltpu.roll(xw, d, axis=0)[HALO:] * w[CONV_K - 1 - d:CONV_K - d, :]
            return _silu(y)

        def l2n(x):
            return x * lax.rsqrt(jnp.sum(x * x, axis=-1, keepdims=True) + EPS)

        qn_s[u, r0:r0 + GR, :] = l2n(conv_silu(q_ref, cwq_ref[...], DK_C)) * (DK_C ** -0.5)
        kn_s[u, r0:r0 + GR, :] = l2n(conv_silu(k_ref, cwk_ref[...], DK_C))
        va_s[u, r0:r0 + GR, :] = conv_silu(v_ref, cwv_ref[...], VW)
        lane = lax.broadcasted_iota(I32, (GR, LANES), 1)
        for j in range(REP):
            hv = (g * U + u) * REP + j
            bcol = jnp.sum(jnp.where(lane == hv, beta_ref[r0:r0 + GR, :], 0.0), axis=-1, keepdims=True)
            gcol = jnp.sum(jnp.where(lane == hv + H_V, gc_ref[r0:r0 + GR, :], 0.0), axis=-1, keepdims=True)
            bb_s[u, j, r0:r0 + GR, :] = jnp.broadcast_to(bcol, (GR, LANES))
            gb_s[u, j, r0:r0 + GR, :] = jnp.broadcast_to(gcol, (GR, LANES))

    assert REP * C == LANES
    ii = lax.broadcasted_iota(I32, (REP * C, REP * C), 0)
    jj = lax.broadcasted_iota(I32, (REP * C, REP * C), 1)
    same_head = (ii >= C) == (jj >= C)
    tri = same_head & (ii >= jj)
    strict = same_head & (ii > jj)
    eye = jnp.where(ii == jj, 1.0, 0.0)

    dot = functools.partial(jnp.dot, preferred_element_type=F32)
    chains = [(c, j) for c in range(GC) for j in range(REP)]

    def row0(gi, c):
        return ((gi % NG) * GC + c) * C

    def scan_step(gi, c, states):
        u, r0 = gi // NG, row0(gi, c)
        for j in range(REP):
            i = c * REP + j
            cols = slice(u * VW + j * DV_C, u * VW + (j + 1) * DV_C)
            r = dot(lhs_s[i], states[j].astype(BF16))
            o = r[DK_C:] + au_s[i]
            states[j] = states[j] * gl_s[i, 0:1, :] + ku_s[i] - r[:DK_C]
            o = o * lax.rsqrt(jnp.mean(o * o, axis=-1, keepdims=True) + EPS) * on_ref[...]
            o = o * _silu(z_ref[r0:r0 + C, cols].astype(F32))
            o_ref[r0:r0 + C, cols] = o.astype(o_ref.dtype)

    def stage(gi, pending):
        pending = list(pending)

        def tick():
            if pending:
                pending.pop(0)()
        u = gi // NG
        r0s = [row0(gi, c) for c in range(GC)]
        qc = [qn_s[u, r0s[c]:r0s[c] + C, :] for c in range(GC)]
        kc = [kn_s[u, r0s[c]:r0s[c] + C, :] for c in range(GC)]
        qkk = [lax.dot_general(jnp.concatenate([qc[c], kc[c]], axis=0).astype(BF16),
                               jnp.concatenate([kc[c], kc[c]], axis=0).astype(BF16),
                               NT, preferred_element_type=F32) for c in range(GC)]
        tick()
        gcb, bb, p, attn = {}, {}, [], []
        for c in range(GC):
            for j in range(REP):
                gcb[c, j] = gb_s[u, j, r0s[c]:r0s[c] + C, :]
                bb[c, j] = bb_s[u, j, r0s[c]:r0s[c] + C, :]
            g2 = jnp.concatenate([gcb[c, j] for j in range(REP)], axis=0)
            b2 = jnp.concatenate([bb[c, j] for j in range(REP)], axis=0)
            gcr = jnp.sum(jnp.where(ii == jj, g2, 0.0), axis=0, keepdims=True)
            decay = jnp.exp(jnp.where(tri, g2 - gcr, -jnp.inf))
            kk2 = jnp.concatenate([qkk[c][C:]] * REP, axis=0)
            qk2 = jnp.concatenate([qkk[c][:C]] * REP, axis=0)
            p.append(jnp.where(strict, -(kk2 * b2 * decay), 0.0))
            attn.append(qk2 * decay)
        qm = [dot(p[c].astype(BF16), p[c].astype(BF16)) for c in range(GC)]
        tick()
        sm = [eye + p[c] for c in range(GC)]
        for _ in range(4):
            r = [dot(qm[c].astype(BF16), jnp.concatenate([qm[c], sm[c]], axis=1).astype(BF16)) for c in range(GC)]
            tick()
            qm = [r[c][:, :REP * C] for c in range(GC)]
            sm = [sm[c] + r[c][:, REP * C:] for c in range(GC)]
        tinv = [sm[c] + dot(qm[c].astype(BF16), sm[c].astype(BF16)) for c in range(GC)]
        tick()
        uw2, glast = [], {}
        for c in range(GC):
            rhs = []
            for j in range(REP):
                vb = va_s[u, r0s[c]:r0s[c] + C, j * DV_C:(j + 1) * DV_C] * bb[c, j]
                kbg = kc[c] * bb[c, j] * jnp.exp(gcb[c, j])
                rhs.append(jnp.concatenate([vb, kbg], axis=1))
                glast[c, j] = gcb[c, j][C - 1:C, :]
            uw2.append(dot(tinv[c].astype(BF16), jnp.concatenate(rhs, axis=0).astype(BF16)))
        tick()
        uw = {(c, j): uw2[c][j * C:(j + 1) * C] for c, j in chains}
        ktuw = {(c, j): lax.dot_general(kc[c] * jnp.exp(glast[c, j] - gcb[c, j]), uw[c, j], TN,
                                        preferred_element_type=F32) for c, j in chains}
        tick()
        atuw2 = [dot(attn[c].astype(BF16), uw2[c].astype(BF16)) for c in range(GC)]
        atuw = {(c, j): atuw2[c][j * C:(j + 1) * C] for c, j in chains}
        while pending:
            tick()
        for c, j in chains:
            i = c * REP + j
            lhs_s[i] = jnp.concatenate([ktuw[c, j][:, DV_C:], qc[c] * jnp.exp(gcb[c, j]) - atuw[c, j][:, DV_C:]],
                                       axis=0).astype(BF16)
            au_s[i] = atuw[c, j][:, :DV_C]
            ku_s[i] = ktuw[c, j][:, :DV_C]
            gl_s[i] = jnp.broadcast_to(jnp.exp(glast[c, j]), (8, LANES))

    states = [[jnp.zeros((DK_C, DV_C), F32) for _ in range(REP)] for _ in range(U)]
    prep(0)
    for gi in range(U * NG):
        if gi + 1 < U * NG:
            prep(gi + 1)
        stage(gi, [functools.partial(scan_step, gi - 1, c, states[(gi - 1) // NG]) for c in range(GC)] if gi else [])
    for c in range(GC):
        scan_step(U * NG - 1, c, states[U - 1])


def _gdn(proj, conv_w, a_log, dt_bias, out_norm):
    B, S, _ = proj.shape
    C = DN_CHUNK
    REP = H_V // H_QK
    VW = REP * DV_C
    GC = 8
    U = 2
    assert S % (C * GC) == 0 and H_QK % U == 0
    beta_all, gc_all = _gdn_gates(proj, a_log, dt_bias)
    kern = functools.partial(_gdn_kernel, S=S, C=C, GC=GC, U=U)
    KW, UW = U * DK_C, U * VW
    return pl.pallas_call(
        kern,
        grid=(B, H_QK // U),
        in_specs=[
            pl.BlockSpec((None, S, KW), lambda b, g: (b, 0, DN_Q // KW + g)),
            pl.BlockSpec((None, S, KW), lambda b, g: (b, 0, DN_K // KW + g)),
            pl.BlockSpec((None, S, UW), lambda b, g: (b, 0, DN_V // UW + g)),
            pl.BlockSpec((None, S, UW), lambda b, g: (b, 0, DN_Z // UW + g)),
            pl.BlockSpec((None, S, LANES), lambda b, g: (b, 0, 0)),
            pl.BlockSpec((None, S, LANES), lambda b, g: (b, 0, 0)),
            pl.BlockSpec((CONV_K, KW), lambda b, g: (0, DN_Q // KW + g)),
            pl.BlockSpec((CONV_K, KW), lambda b, g: (0, DN_K // KW + g)),
            pl.BlockSpec((CONV_K, UW), lambda b, g: (0, DN_V // UW + g)),
            pl.BlockSpec((1, DV_C), lambda b, g: (0, 0)),
        ],
        out_specs=pl.BlockSpec((None, S, UW), lambda b, g: (b, 0, g)),
        out_shape=jax.ShapeDtypeStruct((B, S, MIX_C), BF16),
        scratch_shapes=[
            pltpu.VMEM((U, S, DK_C), F32),
            pltpu.VMEM((U, S, DK_C), F32),
            pltpu.VMEM((U, S, VW), F32),
            pltpu.VMEM((U, REP, S, LANES), F32),
            pltpu.VMEM((U, REP, S, LANES), F32),
            pltpu.VMEM((GC * REP, DK_C + C, DK_C), BF16),
            pltpu.VMEM((GC * REP, C, DV_C), F32),
            pltpu.VMEM((GC * REP, DK_C, DV_C), F32),
            pltpu.VMEM((GC * REP, 8, LANES), F32),
        ],
        compiler_params=pltpu.CompilerParams(
            dimension_semantics=("parallel", "arbitrary"), vmem_limit_bytes=VMEM_LIMIT),
        name="gated_deltanet",
    )(proj, proj, proj, proj, beta_all, gc_all, conv_w, conv_w, conv_w, out_norm.reshape(1, DV_C))


def _dn_out_kernel(o_in_ref, x_ref, w_ref, o_ref):
    o_ref[...] = x_ref[...] + jnp.dot(o_in_ref[...], w_ref[...], preferred_element_type=F32)


def _dn_out(o2d, x2d, w, tm=1024, tn=1024):
    T, D = x2d.shape
    K = o2d.shape[1]
    return pl.pallas_call(
        _dn_out_kernel,
        grid=(T // tm, D // tn),
        in_specs=[
            pl.BlockSpec((tm, K), lambda i, j: (i, 0)),
            pl.BlockSpec((tm, tn), lambda i, j: (i, j)),
            pl.BlockSpec((K, tn), lambda i, j: (0, j)),
        ],
        out_specs=pl.BlockSpec((tm, tn), lambda i, j: (i, j)),
        out_shape=jax.ShapeDtypeStruct((T, D), F32),
        compiler_params=pltpu.CompilerParams(
            dimension_semantics=("parallel", "arbitrary"), vmem_limit_bytes=VMEM_LIMIT),
        name="dn_out_proj",
    )(o2d, x2d, w)


def _ab_weight(w):
    D = w.shape[0]
    w = w.astype(BF16)
    o = np.cumsum([0, MIX_A, D_C, H_I * D_I, D_I, H_I, H_B * DK_B, H_B * DK_B, MIX_B, MIX_AB])
    q_a, c, q_i, k_i, w_i, q_b, k_b, v_b, gate = [w[:, o[i]:o[i + 1]] for i in range(9)]
    pad_kw = jnp.zeros((D, LANES - D_I - H_I), w.dtype)
    pad_end = jnp.zeros((D, AB_NP - AB_KW - LANES), w.dtype)
    return jnp.concatenate([q_a, v_b, gate, q_i, q_b, k_b, c, k_i, w_i, pad_kw, pad_end], axis=1)


def _dn_weight(w):
    return jnp.pad(w.astype(BF16), ((0, 0), (0, DN_NP - w.shape[1])))


def kernel(x, ab_norm, ab_w_in, ab_kv_norm, ab_w_uk, ab_w_uv, ab_q_norm, ab_k_norm, ab_ret_norm, ab_w_out,
           dn_norm, dn_w_in, dn_conv, dn_a_log, dn_dt_bias, dn_out_norm, dn_w_out):
    B, S, D = x.shape
    T = B * S
    depth = ab_norm.shape[0] + dn_norm.shape[0]
    h = x.astype(F32).reshape(T, D)
    for layer in range(depth):
        j = layer // 2
        if layer % 2 == 0:
            proj = _norm_proj(h, ab_norm[j].reshape(1, D), _ab_weight(ab_w_in[j]), tm=1024, tn=2048)
            proj3 = proj.reshape(B, S, AB_NP)
            a_out = _dsa(proj3, ab_kv_norm[j].reshape(1, D_C), ab_w_uk[j].astype(BF16),
                         ab_k_norm[j].reshape(1, DH_A), ab_q_norm[j].reshape(1, DH_A), ab_w_uv[j].astype(BF16))
            b_out = _retention(proj3, ab_ret_norm[j])
            h = _ab_out(a_out.reshape(T, MIX_A), b_out.reshape(T, MIX_B), h, ab_w_out[j].astype(BF16))
        else:
            proj = _norm_proj(h, dn_norm[j].reshape(1, D), _dn_weight(dn_w_in[j]), tm=1024, tn=2560, out_dtype=BF16)
            o = _gdn(proj.reshape(B, S, DN_NP), dn_conv[j], dn_a_log[j], dn_dt_bias[j], dn_out_norm[j])
            h = _dn_out(o.reshape(T, MIX_C), h, dn_w_out[j].astype(BF16))
    return h.reshape(B, S, D).astype(x.dtype)
```

```python
import functools

import numpy as np
import jax
import jax.numpy as jnp
from jax import lax
from jax.experimental import pallas as pl
from jax.experimental.pallas import tpu as pltpu

F32 = jnp.float32
BF16 = jnp.bfloat16
I32 = jnp.int32
I16 = jnp.int16

EPS = 1e-6
H_A, DH_A, D_C, H_I, D_I = 8, 128, 256, 8, 64
INDEX_TOPK = 256
H_B, DK_B, DV_B, RET_CHUNK, RET_THETA = 4, 128, 256, 128, 10000.0
H_QK, H_V, DK_C, DV_C, CONV_K, DN_CHUNK = 16, 32, 128, 128, 4, 64

MIX_A = H_A * DH_A
MIX_B = H_B * DV_B
MIX_AB = MIX_A + MIX_B
MIX_C = H_V * DV_C

LANES = 128
INT_MIN = np.int32(-2**31)

AB_QA, AB_VB, AB_GATE, AB_QI, AB_QB, AB_KB, AB_C, AB_KW, AB_NP = 0, 1024, 2048, 4096, 4608, 5120, 5632, 5888, 6144
DN_Q, DN_K, DN_V, DN_Z, DN_NP = 0, 2048, 4096, 8192, 12288

VMEM_LIMIT = 56 * 1024 * 1024


def _sigmoid(x):
    return 1.0 / (1.0 + jnp.exp(-x))


def _silu(x):
    return x * (0.5 * jnp.tanh(0.5 * x) + 0.5)


def _norm_proj_kernel(x_ref, g_ref, w_ref, o_ref, xn_ref):
    @pl.when(pl.program_id(1) == 0)
    def _():
        x = x_ref[...]
        ms = jnp.mean(x * x, axis=-1, keepdims=True)
        xn_ref[...] = (x * lax.rsqrt(ms + EPS) * g_ref[...]).astype(BF16)

    o_ref[...] = jnp.dot(xn_ref[...], w_ref[...], preferred_element_type=F32).astype(o_ref.dtype)


def _norm_proj(x2d, gain, w, tm, tn, out_dtype=F32):
    T, D = x2d.shape
    N = w.shape[1]
    return pl.pallas_call(
        _norm_proj_kernel,
        grid=(T // tm, N // tn),
        in_specs=[
            pl.BlockSpec((tm, D), lambda i, j: (i, 0)),
            pl.BlockSpec((1, D), lambda i, j: (0, 0)),
            pl.BlockSpec((D, tn), lambda i, j: (0, j)),
        ],
        out_specs=pl.BlockSpec((tm, tn), lambda i, j: (i, j)),
        out_shape=jax.ShapeDtypeStruct((T, N), out_dtype),
        scratch_shapes=[pltpu.VMEM((tm, D), BF16)],
        compiler_params=pltpu.CompilerParams(
            dimension_semantics=("parallel", "arbitrary"), vmem_limit_bytes=VMEM_LIMIT),
        name="norm_proj",
    )(x2d, gain, w)


def _norm_proj_side_kernel(x_ref, g_ref, w_ref, ws_ref, o_ref, os_ref, xn_ref):
    @pl.when(pl.program_id(1) == 0)
    def _():
        x = x_ref[...]
        ms = jnp.mean(x * x, axis=-1, keepdims=True)
        xn_ref[...] = (x * lax.rsqrt(ms + EPS) * g_ref[...]).astype(BF16)
        os_ref[...] = jnp.dot(xn_ref[...], ws_ref[...], preferred_element_type=F32).astype(os_ref.dtype)

    o_ref[...] = jnp.dot(xn_ref[...], w_ref[...], preferred_element_type=F32).astype(o_ref.dtype)


def _norm_proj_side(x2d, gain, w, w_side, tm, tn, out_dtype):
    T, D = x2d.shape
    N, NS = w.shape[1], w_side.shape[1]
    return pl.pallas_call(
        _norm_proj_side_kernel,
        grid=(T // tm, N // tn),
        in_specs=[
            pl.BlockSpec((tm, D), lambda i, j: (i, 0)),
            pl.BlockSpec((1, D), lambda i, j: (0, 0)),
            pl.BlockSpec((D, tn), lambda i, j: (0, j)),
            pl.BlockSpec((D, NS), lambda i, j: (0, 0)),
        ],
        out_specs=[pl.BlockSpec((tm, tn), lambda i, j: (i, j)), pl.BlockSpec((tm, NS), lambda i, j: (i, 0))],
        out_shape=[jax.ShapeDtypeStruct((T, N), out_dtype), jax.ShapeDtypeStruct((T, NS), out_dtype)],
        scratch_shapes=[pltpu.VMEM((tm, D), BF16)],
        compiler_params=pltpu.CompilerParams(
            dimension_semantics=("parallel", "arbitrary"), vmem_limit_bytes=VMEM_LIMIT),
        name="norm_proj_side",
    )(x2d, gain, w, w_side)


def _dsa_kernel(qa_ref, c_ref, qi_ref, kwq_ref, kwk_ref, gate_ref, kvn_ref, wuk_ref, kn_ref, qn_ref, wuv_ref,
                o_ref, cnT_s, k_s, ki_s, key_s, khi_s, klo_s, bias_s, lg_s, j_s, qall_s, lat_s, *, S, Q, topk):
    qb = pl.program_id(1)
    nkb = qb + 1
    RB = 128
    nrb = nkb * (Q // RB)
    NT = (((1,), (1,)), ((), ()))

    @pl.when(qb == 0)
    def _prep():
        def body(j, _):
            r0 = pl.multiple_of(j * Q, Q)
            c = c_ref[pl.ds(r0, Q), :]
            cn = c * lax.rsqrt(jnp.mean(c * c, axis=-1, keepdims=True) + EPS) * kvn_ref[...]
            kk = jnp.dot(cn.astype(BF16), wuk_ref[...], preferred_element_type=F32)
            kk = kk * lax.rsqrt(jnp.mean(kk * kk, axis=-1, keepdims=True) + EPS) * kn_ref[...]
            cnT_s[j] = cn.T.astype(BF16)
            k_s[pl.ds(r0, Q), :] = kk.astype(BF16)
            ki_s[pl.ds(r0, Q), :] = kwk_ref[pl.ds(r0, Q), 0:D_I].astype(BF16)
            return 0
        lax.fori_loop(0, S // Q, body, 0)

    tpos = qb * Q + lax.broadcasted_iota(I32, (1, Q), 1)
    w_t = kwq_ref[...].T[D_I:D_I + H_I, :] * (H_I ** -0.5 * D_I ** -0.5)

    qi_heads = [qi_ref[:, h * D_I:(h + 1) * D_I].astype(BF16) for h in range(H_I)]

    def score_blk(j, _):
        r0 = pl.multiple_of(j * Q, Q)
        kib = ki_s[pl.ds(r0, Q), :]
        acc = jnp.zeros((Q, Q), F32)
        for h in range(H_I):
            s = lax.dot_general(kib, qi_heads[h], NT, preferred_element_type=F32)
            acc = acc + jnp.maximum(s, 0.0) * w_t[h:h + 1, :]
        acc = acc + 0.0
        bits = pltpu.bitcast(acc, I32)
        key = bits ^ ((bits >> 31) & np.int32(0x7FFFFFFF))
        kpos = r0 + lax.broadcasted_iota(I32, (Q, Q), 0)
        key = jnp.where(kpos <= tpos, key, INT_MIN)
        key_s[pl.ds(r0, Q), :] = key
        khi_s[pl.ds(r0, Q), :] = (key >> 16).astype(I16)
        return 0
    lax.fori_loop(0, nkb, score_blk, 0)

    def count(pred):
        def body(i, acc):
            r0 = pl.multiple_of(i * RB, RB)
            kpos = r0 + lax.broadcasted_iota(I32, (RB, Q), 0)
            return acc + jnp.where(pred(key_s[pl.ds(r0, RB), :], kpos), 1.0, 0.0)
        acc = lax.fori_loop(0, nrb, body, jnp.zeros((RB, Q), F32))
        return jnp.sum(acc, axis=0, keepdims=True)

    def count16(ref, pred):
        def body(j, acc):
            r0 = pl.multiple_of(j * Q, Q)
            return acc + jnp.where(pred(ref[pl.ds(r0, Q), :]), np.int16(1), np.int16(0))
        acc = lax.fori_loop(0, nkb, body, jnp.zeros((Q, Q), I16))
        acc = jnp.sum(acc.reshape(Q // 16, 16, Q), axis=0)
        return jnp.sum(acc.astype(F32), axis=0, keepdims=True)

    def search16(ref, base):
        def bit_body(i, prefix_u):
            cand_u = prefix_u | jnp.left_shift(np.int32(1), 15 - i)
            cand_s = (cand_u - HALF16).astype(I16)
            cnt = base + count16(ref, lambda k: k >= cand_s)
            return jnp.where(cnt >= float(topk), cand_u, prefix_u)
        return lax.fori_loop(0, 16, bit_body, jnp.zeros((1, Q), I32))

    HALF16 = np.int32(1 << 15)
    th = search16(khi_s, 0.0) - HALF16
    th16 = th.astype(I16)
    cnt_hi_gt = count16(khi_s, lambda k: k > th16)

    def lo_blk(j, _):
        r0 = pl.multiple_of(j * Q, Q)
        k = key_s[pl.ds(r0, Q), :]
        lo = (k & np.int32(0xFFFF)) - HALF16
        klo_s[pl.ds(r0, Q), :] = jnp.where((k >> 16) == th, lo, -HALF16).astype(I16)
        return 0
    lax.fori_loop(0, nkb, lo_blk, 0)
    tl_u = search16(klo_s, cnt_hi_gt)
    thr = (th << 16) | tl_u

    tl16 = (tl_u - HALF16).astype(I16)
    cnt_ge = cnt_hi_gt + count16(klo_s, lambda k: k >= tl16)
    ambiguous = (cnt_ge != float(topk)) & (tpos >= topk - 1)
    j_s[...] = jnp.full((1, Q), S, I32)

    @pl.when(jnp.max(jnp.where(ambiguous, 1.0, 0.0)) > 0.0)
    def _ties():
        need = float(topk) - count(lambda k, _: k > thr)
        nbits = int(S).bit_length() - 1

        def jbit(i, prefix):
            cand = prefix | jnp.left_shift(np.int32(1), nbits - 1 - i)
            cnt = count(lambda k, kpos: (k == thr) & (kpos < cand))
            return jnp.where(cnt < need, cand, prefix)
        jstar = lax.fori_loop(0, nbits, jbit, jnp.zeros((1, Q), I32))
        j_s[...] = jnp.where(ambiguous, jstar, S)

    jsel = j_s[...]

    def bias_blk(i, _):
        r0 = pl.multiple_of(i * RB, RB)
        k = key_s[pl.ds(r0, RB), :]
        kpos = r0 + lax.broadcasted_iota(I32, (RB, Q), 0)
        sel = (kpos <= tpos) & ((k > thr) | ((k == thr) & (kpos <= jsel)))
        bias_s[pl.ds(r0, RB), :] = jnp.where(sel, 0.0, -jnp.inf)
        return 0
    lax.fori_loop(0, nrb, bias_blk, 0)

    for h in range(H_A):
        qh = qa_ref[:, h * DH_A:(h + 1) * DH_A]
        qh = qh * lax.rsqrt(jnp.mean(qh * qh, axis=-1, keepdims=True) + EPS) * qn_ref[...] * (DH_A ** -0.5)
        qall_s[h * Q:(h + 1) * Q, :] = qh.astype(BF16)

    def pass_a(j, m):
        r0 = pl.multiple_of(j * Q, Q)
        s = lax.dot_general(k_s[pl.ds(r0, Q), :], qall_s[...], NT, preferred_element_type=F32)
        bias = bias_s[pl.ds(r0, Q), :]
        ms = []
        for h in range(H_A):
            sh = s[:, h * Q:(h + 1) * Q] + bias
            lg_s[pl.ds(r0, Q), h * Q:(h + 1) * Q] = sh
            ms.append(jnp.max(sh, axis=0, keepdims=True))
        return jnp.maximum(m, jnp.concatenate(ms, axis=1))
    m = lax.fori_loop(0, nkb, pass_a, jnp.full((1, H_A * Q), -jnp.inf, F32))

    lat_s[...] = jnp.zeros_like(lat_s)

    def pass_b(j, l):
        r0 = pl.multiple_of(j * Q, Q)
        p = jnp.exp(lg_s[pl.ds(r0, Q), :] - m)
        lat_s[...] += jnp.dot(cnT_s[j], p.astype(BF16), preferred_element_type=F32)
        return l + jnp.sum(p, axis=0, keepdims=True)
    l = lax.fori_loop(0, nkb, pass_b, jnp.zeros((1, H_A * Q), F32))

    lat = (lat_s[...] * (1.0 / l)).astype(BF16)
    for h in range(H_A):
        cols = slice(h * DH_A, (h + 1) * DH_A)
        out_h = lax.dot_general(lat[:, h * Q:(h + 1) * Q], wuv_ref[h], (((0,), (0,)), ((), ())),
                                preferred_element_type=F32)
        o_ref[:, cols] = (out_h * _silu(gate_ref[:, cols])).astype(o_ref.dtype)


def _dsa(proj, kv_norm, w_uk, k_norm, q_norm, w_uv, Q=256):
    B, S, _ = proj.shape
    topk = min(INDEX_TOPK, S // 4)
    assert S % Q == 0 and topk <= Q and (S & (S - 1)) == 0
    kern = functools.partial(_dsa_kernel, S=S, Q=Q, topk=topk)
    full = lambda *shape: pl.BlockSpec(shape, lambda b, q: (0,) * len(shape))
    return pl.pallas_call(
        kern,
        grid=(B, S // Q),
        in_specs=[
            pl.BlockSpec((None, Q, MIX_A), lambda b, q: (b, q, AB_QA // MIX_A)),
            pl.BlockSpec((None, S, D_C), lambda b, q: (b, 0, AB_C // D_C)),
            pl.BlockSpec((None, Q, H_I * D_I), lambda b, q: (b, q, AB_QI // (H_I * D_I))),
            pl.BlockSpec((None, Q, LANES), lambda b, q: (b, q, AB_KW // LANES)),
            pl.BlockSpec((None, S, LANES), lambda b, q: (b, 0, AB_KW // LANES)),
            pl.BlockSpec((None, Q, MIX_A), lambda b, q: (b, q, AB_GATE // MIX_A)),
            full(1, D_C), full(D_C, DH_A), full(1, DH_A), full(1, DH_A), full(H_A, D_C, DH_A),
        ],
        out_specs=pl.BlockSpec((None, Q, MIX_A), lambda b, q: (b, q, 0)),
        out_shape=jax.ShapeDtypeStruct((B, S, MIX_A), BF16),
        scratch_shapes=[
            pltpu.VMEM((S // Q, D_C, Q), BF16),
            pltpu.VMEM((S, DH_A), BF16),
            pltpu.VMEM((S, D_I), BF16),
            pltpu.VMEM((S, Q), I32),
            pltpu.VMEM((S, Q), I16),
            pltpu.VMEM((S, Q), I16),
            pltpu.VMEM((S, Q), F32),
            pltpu.VMEM((S, H_A * Q), F32),
            pltpu.VMEM((1, Q), I32),
            pltpu.VMEM((H_A * Q, DH_A), BF16),
            pltpu.VMEM((D_C, H_A * Q), F32),
        ],
        compiler_params=pltpu.CompilerParams(
            dimension_semantics=("parallel", "arbitrary"), vmem_limit_bytes=VMEM_LIMIT),
        name="dsa_attention",
    )(proj, proj, proj, proj, proj, proj, kv_norm, w_uk, k_norm, q_norm, w_uv)


def _ret_kernel(lg_ref, q_ref, k_ref, v_ref, gate_ref, cos_ref, sin_ref, g_ref, o_ref, *, S, C):
    h = pl.program_id(1)
    lg = lg_ref[h]
    NT = (((1,), (1,)), ((), ()))
    TN = (((0,), (0,)), ((), ()))
    N = S // C
    ii = lax.broadcasted_iota(I32, (C, C), 0)
    jj = lax.broadcasted_iota(I32, (C, C), 1)
    diff = (ii - jj).astype(F32)
    dmask = jnp.where(diff >= 0, jnp.exp(jnp.maximum(diff, 0.0) * lg), 0.0)
    icol = lax.broadcasted_iota(I32, (C, 1), 0).astype(F32)
    k_dec = jnp.exp((C - 1 - icol) * lg)
    q_dec = jnp.exp((icol + 1.0) * lg)
    chunk_decay = jnp.exp(jnp.full((1, DV_B), float(C), F32) * lg)

    def rot(x, cos, sin):
        return x * cos + pltpu.roll(x, DK_B // 2, axis=1) * sin

    q, k, v = [], [], []
    for n in range(N):
        rows = slice(n * C, (n + 1) * C)
        cos, sin = cos_ref[rows, :], sin_ref[rows, :]
        q.append(rot(q_ref[rows, :], cos, sin))
        k.append(rot(k_ref[rows, :], cos, sin) * (DK_B ** -0.5))
        v.append(v_ref[rows, :])
    inner = [lax.dot_general(q[n], k[n], NT, preferred_element_type=F32) * dmask for n in range(N)]
    kv = [lax.dot_general(k[n] * k_dec, v[n], TN, preferred_element_type=F32) for n in range(N - 1)]
    out = [jnp.dot(inner[n], v[n], preferred_element_type=F32) for n in range(N)]
    st = kv[0]
    for n in range(1, N):
        out[n] = out[n] + jnp.dot(q[n] * q_dec, st, preferred_element_type=F32)
        if n + 1 < N:
            st = st * chunk_decay + kv[n]
    for n in range(N):
        rows = slice(n * C, (n + 1) * C)
        o = out[n] * lax.rsqrt(jnp.mean(out[n] * out[n], axis=-1, keepdims=True) + EPS) * g_ref[...]
        o_ref[rows, :] = (o * _silu(gate_ref[rows, :])).astype(o_ref.dtype)


def _retention(proj, ret_norm):
    B, S, _ = proj.shape
    C = RET_CHUNK
    pos = jnp.arange(S, dtype=F32)
    inv_freq = 1.0 / (RET_THETA ** jnp.linspace(0.0, 1.0, DK_B // 2, dtype=F32))
    ang = pos[:, None] * inv_freq[None, :]
    cos2 = jnp.concatenate([jnp.cos(ang), jnp.cos(ang)], axis=-1)
    sin2 = jnp.concatenate([-jnp.sin(ang), jnp.sin(ang)], axis=-1)
    log_gamma = jnp.log1p(-jnp.exp2(-5.0 - jnp.arange(H_B, dtype=F32)))
    kern = functools.partial(_ret_kernel, S=S, C=C)
    return pl.pallas_call(
        kern,
        grid_spec=pltpu.PrefetchScalarGridSpec(
            num_scalar_prefetch=1,
            grid=(B, H_B),
            in_specs=[
                pl.BlockSpec((None, S, DK_B), lambda b, h, lg: (b, 0, AB_QB // DK_B + h)),
                pl.BlockSpec((None, S, DK_B), lambda b, h, lg: (b, 0, AB_KB // DK_B + h)),
                pl.BlockSpec((None, S, DV_B), lambda b, h, lg: (b, 0, AB_VB // DV_B + h)),
                pl.BlockSpec((None, S, DV_B), lambda b, h, lg: (b, 0, (AB_GATE + MIX_A) // DV_B + h)),
                pl.BlockSpec((S, DK_B), lambda b, h, lg: (0, 0)),
                pl.BlockSpec((S, DK_B), lambda b, h, lg: (0, 0)),
                pl.BlockSpec((None, 1, DV_B), lambda b, h, lg: (h, 0, 0)),
            ],
            out_specs=pl.BlockSpec((None, S, DV_B), lambda b, h, lg: (b, 0, h)),
        ),
        out_shape=jax.ShapeDtypeStruct((B, S, MIX_B), BF16),
        compiler_params=pltpu.CompilerParams(
            dimension_semantics=("parallel", "arbitrary"), vmem_limit_bytes=VMEM_LIMIT),
        name="retention",
    )(log_gamma, proj, proj, proj, proj, cos2, sin2, ret_norm.reshape(H_B, 1, DV_B))


def _ab_out_kernel(a_ref, b_ref, x_ref, w_ref, o_ref):
    o_ref[...] = (x_ref[...] + jnp.dot(a_ref[...], w_ref[:MIX_A, :], preferred_element_type=F32)
                  + jnp.dot(b_ref[...], w_ref[MIX_A:, :], preferred_element_type=F32))


def _ab_out(a2d, b2d, x2d, w, tm=1024, tn=1024):
    T, D = x2d.shape
    return pl.pallas_call(
        _ab_out_kernel,
        grid=(T // tm, D // tn),
        in_specs=[
            pl.BlockSpec((tm, MIX_A), lambda i, j: (i, 0)),
            pl.BlockSpec((tm, MIX_B), lambda i, j: (i, 0)),
            pl.BlockSpec((tm, tn), lambda i, j: (i, j)),
            pl.BlockSpec((MIX_AB, tn), lambda i, j: (0, j)),
        ],
        out_specs=pl.BlockSpec((tm, tn), lambda i, j: (i, j)),
        out_shape=jax.ShapeDtypeStruct((T, D), F32),
        compiler_params=pltpu.CompilerParams(
            dimension_semantics=("parallel", "arbitrary"), vmem_limit_bytes=VMEM_LIMIT),
        name="ab_out_proj",
    )(a2d, b2d, x2d, w)


def _gdn_gates_kernel(ba_ref, alog_ref, dt_ref, beta_ref, gc_ref, *, S, C):
    ba = ba_ref[...].astype(F32)
    beta_ref[...] = _sigmoid(ba)
    xg = ba + dt_ref[...]
    softplus = jnp.maximum(xg, 0.0) + jnp.log(1.0 + jnp.exp(-jnp.abs(xg)))
    g_all = -jnp.exp(alog_ref[...]) * softplus
    rc = lax.broadcasted_iota(I32, (S, LANES), 0) & (C - 1)
    sh = 1
    while sh < C:
        g_all = g_all + jnp.where(rc >= sh, pltpu.roll(g_all, sh, axis=0), 0.0)
        sh *= 2
    gc_ref[...] = g_all


def _gdn_gates(ba, a_log, dt_bias):
    B, S, _ = ba.shape
    alog_row = jnp.zeros((1, LANES), F32).at[0, H_V:2 * H_V].set(a_log)
    dt_row = jnp.zeros((1, LANES), F32).at[0, H_V:2 * H_V].set(dt_bias)
    row = pl.BlockSpec((1, LANES), lambda b: (0, 0))
    out = pl.BlockSpec((None, S, LANES), lambda b: (b, 0, 0))
    return pl.pallas_call(
        functools.partial(_gdn_gates_kernel, S=S, C=DN_CHUNK),
        grid=(B,),
        in_specs=[pl.BlockSpec((None, S, LANES), lambda b: (b, 0, 0)), row, row],
        out_specs=[out, out],
        out_shape=[jax.ShapeDtypeStruct((B, S, LANES), F32)] * 2,
        compiler_params=pltpu.CompilerParams(dimension_semantics=("parallel",), vmem_limit_bytes=VMEM_LIMIT),
        name="gdn_gates",
    )(ba, alog_row, dt_row)


def _gdn_kernel(q_ref, k_ref, v_ref, z_ref, beta_ref, gc_ref, cwq_ref, cwk_ref, cwv_ref, on_ref,
                o_ref, qn_s, kn_s, va_s, bb_s, gb_s, lhs_s, au_s, ku_s, gl_s, *, S, C, GC, U):
    g = pl.program_id(1)
    NG = S // (C * GC)
    REP = H_V // H_QK
    VW = REP * DV_C
    NT = (((1,), (1,)), ((), ()))
    TN = (((0,), (0,)), ((), ()))

    GR = GC * C
    HALO = 16

    def prep(gi):
        u, r0 = gi // NG, (gi % NG) * GR

        def conv_silu(ref, w, width):
            cols = slice(u * width, (u + 1) * width)
            w = w[:, cols]
            if r0 == 0:
                xw = jnp.concatenate([jnp.zeros((HALO, width), F32), ref[0:GR, cols].astype(F32)], axis=0)
            else:
                xw = ref[r0 - HALO:r0 + GR, cols].astype(F32)
            y = xw[HALO:] * w[CONV_K - 1:CONV_K, :]
            for d in range(1, CONV_K):
                y = y + pltpu.roll(xw, d, axis=0)[HALO:] * w[CONV_K - 1 - d:CONV_K - d, :]
            return _silu(y)

        def l2n(x):
            return x * lax.rsqrt(jnp.sum(x * x, axis=-1, keepdims=True) + EPS)

        qn_s[u, r0:r0 + GR, :] = l2n(conv_silu(q_ref, cwq_ref[...], DK_C)) * (DK_C ** -0.5)
        kn_s[u, r0:r0 + GR, :] = l2n(conv_silu(k_ref, cwk_ref[...], DK_C))
        va_s[u, r0:r0 + GR, :] = conv_silu(v_ref, cwv_ref[...], VW)
        lane = lax.broadcasted_iota(I32, (GR, LANES), 1)
        for j in range(REP):
            hv = (g * U + u) * REP + j
            bcol = jnp.sum(jnp.where(lane == hv, beta_ref[r0:r0 + GR, :], 0.0), axis=-1, keepdims=True)
            gcol = jnp.sum(jnp.where(lane == hv + H_V, gc_ref[r0:r0 + GR, :], 0.0), axis=-1, keepdims=True)
            bb_s[u, j, r0:r0 + GR, :] = jnp.broadcast_to(bcol, (GR, LANES))
            gb_s[u, j, r0:r0 + GR, :] = jnp.broadcast_to(gcol, (GR, LANES))

    assert REP * C == LANES
    ii = lax.broadcasted_iota(I32, (REP * C, REP * C), 0)
    jj = lax.broadcasted_iota(I32, (REP * C, REP * C), 1)
    same_head = (ii >= C) == (jj >= C)
    tri = same_head & (ii >= jj)
    strict = same_head & (ii > jj)
    eye = jnp.where(ii == jj, 1.0, 0.0)

    dot = functools.partial(jnp.dot, preferred_element_type=F32)
    chains = [(c, j) for c in range(GC) for j in range(REP)]

    def row0(gi, c):
        return ((gi % NG) * GC + c) * C

    def scan_step(gi, c, states):
        u, r0 = gi // NG, row0(gi, c)
        for j in range(REP):
            i = c * REP + j
            cols = slice(u * VW + j * DV_C, u * VW + (j + 1) * DV_C)
            r = dot(lhs_s[i], states[j].astype(BF16))
            o = r[DK_C:] + au_s[i]
            states[j] = states[j] * gl_s[i, 0:1, :] + ku_s[i] - r[:DK_C]
            o = o * lax.rsqrt(jnp.mean(o * o, axis=-1, keepdims=True) + EPS) * on_ref[...]
            o = o * _silu(z_ref[r0:r0 + C, cols].astype(F32))
            o_ref[r0:r0 + C, cols] = o.astype(o_ref.dtype)

    def stage(gi, pending):
        pending = list(pending)

        def tick():
            if pending:
                pending.pop(0)()
        u = gi // NG
        r0s = [row0(gi, c) for c in range(GC)]
        qc = [qn_s[u, r0s[c]:r0s[c] + C, :] for c in range(GC)]
        kc = [kn_s[u, r0s[c]:r0s[c] + C, :] for c in range(GC)]
        qkk = [lax.dot_general(jnp.concatenate([qc[c], kc[c]], axis=0).astype(BF16),
                               jnp.concatenate([kc[c], kc[c]], axis=0).astype(BF16),
                               NT, preferred_element_type=F32) for c in range(GC)]
        tick()
        gcb, bb, p, attn = {}, {}, [], []
        for c in range(GC):
            for j in range(REP):
                gcb[c, j] = gb_s[u, j, r0s[c]:r0s[c] + C, :]
                bb[c, j] = bb_s[u, j, r0s[c]:r0s[c] + C, :]
            g2 = jnp.concatenate([gcb[c, j] for j in range(REP)], axis=0)
            b2 = jnp.concatenate([bb[c, j] for j in range(REP)], axis=0)
            gcr = jnp.sum(jnp.where(ii == jj, g2, 0.0), axis=0, keepdims=True)
            decay = jnp.exp(jnp.where(tri, g2 - gcr, -jnp.inf))
            kk2 = jnp.concatenate([qkk[c][C:]] * REP, axis=0)
            qk2 = jnp.concatenate([qkk[c][:C]] * REP, axis=0)
            p.append(jnp.where(strict, -(kk2 * b2 * decay), 0.0))
            attn.append(qk2 * decay)
        qm = [dot(p[c].astype(BF16), p[c].astype(BF16)) for c in range(GC)]
        tick()
        sm = [eye + p[c] for c in range(GC)]
        for _ in range(4):
            r = [dot(qm[c].astype(BF16), jnp.concatenate([qm[c], sm[c]], axis=1).astype(BF16)) for c in range(GC)]
            tick()
            qm = [r[c][:, :REP * C] for c in range(GC)]
            sm = [sm[c] + r[c][:, REP * C:] for c in range(GC)]
        tinv = [sm[c] + dot(qm[c].astype(BF16), sm[c].astype(BF16)) for c in range(GC)]
        tick()
        uw2, glast = [], {}
        for c in range(GC):
            rhs = []
            for j in range(REP):
                vb = va_s[u, r0s[c]:r0s[c] + C, j * DV_C:(j + 1) * DV_C] * bb[c, j]
                kbg = kc[c] * bb[c, j] * jnp.exp(gcb[c, j])
                rhs.append(jnp.concatenate([vb, kbg], axis=1))
                glast[c, j] = gcb[c, j][C - 1:C, :]
            uw2.append(dot(tinv[c].astype(BF16), jnp.concatenate(rhs, axis=0).astype(BF16)))
        tick()
        uw = {(c, j): uw2[c][j * C:(j + 1) * C] for c, j in chains}
        ktuw = {(c, j): lax.dot_general(kc[c] * jnp.exp(glast[c, j] - gcb[c, j]), uw[c, j], TN,
                                        preferred_element_type=F32) for c, j in chains}
        tick()
        atuw2 = [dot(attn[c].astype(BF16), uw2[c].astype(BF16)) for c in range(GC)]
        atuw = {(c, j): atuw2[c][j * C:(j + 1) * C] for c, j in chains}
        while pending:
            tick()
        for c, j in chains:
            i = c * REP + j
            lhs_s[i] = jnp.concatenate([ktuw[c, j][:, DV_C:], qc[c] * jnp.exp(gcb[c, j]) - atuw[c, j][:, DV_C:]],
                                       axis=0).astype(BF16)
            au_s[i] = atuw[c, j][:, :DV_C]
            ku_s[i] = ktuw[c, j][:, :DV_C]
            gl_s[i] = jnp.broadcast_to(jnp.exp(glast[c, j]), (8, LANES))

    states = [[jnp.zeros((DK_C, DV_C), F32) for _ in range(REP)] for _ in range(U)]
    prep(0)
    for gi in range(U * NG):
        if gi + 1 < U * NG:
            prep(gi + 1)
        stage(gi, [functools.partial(scan_step, gi - 1, c, states[(gi - 1) // NG]) for c in range(GC)] if gi else [])
    for c in range(GC):
        scan_step(U * NG - 1, c, states[U - 1])


def _gdn(proj, ba, conv_w, a_log, dt_bias, out_norm):
    B, S, _ = proj.shape
    C = DN_CHUNK
    REP = H_V // H_QK
    VW = REP * DV_C
    GC = 8
    U = 2
    assert S % (C * GC) == 0 and H_QK % U == 0
    beta_all, gc_all = _gdn_gates(ba, a_log, dt_bias)
    kern = functools.partial(_gdn_kernel, S=S, C=C, GC=GC, U=U)
    KW, UW = U * DK_C, U * VW
    return pl.pallas_call(
        kern,
        grid=(B, H_QK // U),
        in_specs=[
            pl.BlockSpec((None, S, KW), lambda b, g: (b, 0, DN_Q // KW + g)),
            pl.BlockSpec((None, S, KW), lambda b, g: (b, 0, DN_K // KW + g)),
            pl.BlockSpec((None, S, UW), lambda b, g: (b, 0, DN_V // UW + g)),
            pl.BlockSpec((None, S, UW), lambda b, g: (b, 0, DN_Z // UW + g)),
            pl.BlockSpec((None, S, LANES), lambda b, g: (b, 0, 0)),
            pl.BlockSpec((None, S, LANES), lambda b, g: (b, 0, 0)),
            pl.BlockSpec((CONV_K, KW), lambda b, g: (0, DN_Q // KW + g)),
            pl.BlockSpec((CONV_K, KW), lambda b, g: (0, DN_K // KW + g)),
            pl.BlockSpec((CONV_K, UW), lambda b, g: (0, DN_V // UW + g)),
            pl.BlockSpec((1, DV_C), lambda b, g: (0, 0)),
        ],
        out_specs=pl.BlockSpec((None, S, UW), lambda b, g: (b, 0, g)),
        out_shape=jax.ShapeDtypeStruct((B, S, MIX_C), BF16),
        scratch_shapes=[
            pltpu.VMEM((U, S, DK_C), F32),
            pltpu.VMEM((U, S, DK_C), F32),
            pltpu.VMEM((U, S, VW), F32),
            pltpu.VMEM((U, REP, S, LANES), F32),
            pltpu.VMEM((U, REP, S, LANES), F32),
            pltpu.VMEM((GC * REP, DK_C + C, DK_C), BF16),
            pltpu.VMEM((GC * REP, C, DV_C), F32),
            pltpu.VMEM((GC * REP, DK_C, DV_C), F32),
            pltpu.VMEM((GC * REP, 8, LANES), F32),
        ],
        compiler_params=pltpu.CompilerParams(
            dimension_semantics=("parallel", "arbitrary"), vmem_limit_bytes=VMEM_LIMIT),
        name="gated_deltanet",
    )(proj, proj, proj, proj, beta_all, gc_all, conv_w, conv_w, conv_w, out_norm.reshape(1, DV_C))


def _dn_out_kernel(o_in_ref, x_ref, w_ref, o_ref):
    o_ref[...] = x_ref[...] + jnp.dot(o_in_ref[...], w_ref[...], preferred_element_type=F32)


def _dn_out(o2d, x2d, w, tm=1024, tn=1024):
    T, D = x2d.shape
    K = o2d.shape[1]
    return pl.pallas_call(
        _dn_out_kernel,
        grid=(T // tm, D // tn),
        in_specs=[
            pl.BlockSpec((tm, K), lambda i, j: (i, 0)),
            pl.BlockSpec((tm, tn), lambda i, j: (i, j)),
            pl.BlockSpec((K, tn), lambda i, j: (0, j)),
        ],
        out_specs=pl.BlockSpec((tm, tn), lambda i, j: (i, j)),
        out_shape=jax.ShapeDtypeStruct((T, D), F32),
        compiler_params=pltpu.CompilerParams(
            dimension_semantics=("parallel", "arbitrary"), vmem_limit_bytes=VMEM_LIMIT),
        name="dn_out_proj",
    )(o2d, x2d, w)


def _ab_weight(w):
    D = w.shape[0]
    w = w.astype(BF16)
    o = np.cumsum([0, MIX_A, D_C, H_I * D_I, D_I, H_I, H_B * DK_B, H_B * DK_B, MIX_B, MIX_AB])
    q_a, c, q_i, k_i, w_i, q_b, k_b, v_b, gate = [w[:, o[i]:o[i + 1]] for i in range(9)]
    pad_kw = jnp.zeros((D, LANES - D_I - H_I), w.dtype)
    pad_end = jnp.zeros((D, AB_NP - AB_KW - LANES), w.dtype)
    return jnp.concatenate([q_a, v_b, gate, q_i, q_b, k_b, c, k_i, w_i, pad_kw, pad_end], axis=1)


def _dn_weight(w):
    side = jnp.pad(w[:, DN_NP:], ((0, 0), (0, LANES - (w.shape[1] - DN_NP))))
    return w[:, :DN_NP].astype(BF16), side.astype(BF16)


def kernel(x, ab_norm, ab_w_in, ab_kv_norm, ab_w_uk, ab_w_uv, ab_q_norm, ab_k_norm, ab_ret_norm, ab_w_out,
           dn_norm, dn_w_in, dn_conv, dn_a_log, dn_dt_bias, dn_out_norm, dn_w_out):
    B, S, D = x.shape
    T = B * S
    depth = ab_norm.shape[0] + dn_norm.shape[0]
    h = x.astype(F32).reshape(T, D)
    for layer in range(depth):
        j = layer // 2
        if layer % 2 == 0:
            proj = _norm_proj(h, ab_norm[j].reshape(1, D), _ab_weight(ab_w_in[j]), tm=1024, tn=2048)
            proj3 = proj.reshape(B, S, AB_NP)
            a_out = _dsa(proj3, ab_kv_norm[j].reshape(1, D_C), ab_w_uk[j].astype(BF16),
                         ab_k_norm[j].reshape(1, DH_A), ab_q_norm[j].reshape(1, DH_A), ab_w_uv[j].astype(BF16))
            b_out = _retention(proj3, ab_ret_norm[j])
            h = _ab_out(a_out.reshape(T, MIX_A), b_out.reshape(T, MIX_B), h, ab_w_out[j].astype(BF16))
        else:
            w_main, w_side = _dn_weight(dn_w_in[j])
            proj, ba = _norm_proj_side(h, dn_norm[j].reshape(1, D), w_main, w_side, tm=1024, tn=2048, out_dtype=BF16)
            o = _gdn(proj.reshape(B, S, DN_NP), ba.reshape(B, S, LANES), dn_conv[j], dn_a_log[j], dn_dt_bias[j],
                     dn_out_norm[j])
            h = _dn_out(o.reshape(T, MIX_C), h, dn_w_out[j].astype(BF16))
    return h.reshape(B, S, D).astype(x.dtype)
```

```python
import functools

import numpy as np
import jax
import jax.numpy as jnp
from jax import lax
from jax.experimental import pallas as pl
from jax.experimental.pallas import tpu as pltpu

F32 = jnp.float32
BF16 = jnp.bfloat16
I32 = jnp.int32
I16 = jnp.int16

EPS = 1e-6
H_A, DH_A, D_C, H_I, D_I = 8, 128, 256, 8, 64
INDEX_TOPK = 256
H_B, DK_B, DV_B, RET_CHUNK, RET_THETA = 4, 128, 256, 128, 10000.0
H_QK, H_V, DK_C, DV_C, CONV_K, DN_CHUNK = 16, 32, 128, 128, 4, 64

MIX_A = H_A * DH_A
MIX_B = H_B * DV_B
MIX_AB = MIX_A + MIX_B
MIX_C = H_V * DV_C

LANES = 128
INT_MIN = np.int32(-2**31)

AB_QA, AB_VB, AB_GATE, AB_QI, AB_QB, AB_KB, AB_C, AB_KW, AB_NP = 0, 1024, 2048, 4096, 4608, 5120, 5632, 5888, 6144
DN_Q, DN_K, DN_V, DN_Z, DN_NP = 0, 2048, 4096, 8192, 12288

VMEM_LIMIT = 56 * 1024 * 1024


def _sigmoid(x):
    return 1.0 / (1.0 + jnp.exp(-x))


def _silu(x):
    return x * (0.5 * jnp.tanh(0.5 * x) + 0.5)


def _norm_proj_kernel(x_ref, g_ref, w_ref, o_ref, xn_ref):
    @pl.when(pl.program_id(1) == 0)
    def _():
        x = x_ref[...]
        ms = jnp.mean(x * x, axis=-1, keepdims=True)
        xn_ref[...] = (x * lax.rsqrt(ms + EPS) * g_ref[...]).astype(BF16)

    o_ref[...] = jnp.dot(xn_ref[...], w_ref[...], preferred_element_type=F32).astype(o_ref.dtype)


def _norm_proj(x2d, gain, w, tm, tn, out_dtype=F32):
    T, D = x2d.shape
    N = w.shape[1]
    return pl.pallas_call(
        _norm_proj_kernel,
        grid=(T // tm, N // tn),
        in_specs=[
            pl.BlockSpec((tm, D), lambda i, j: (i, 0)),
            pl.BlockSpec((1, D), lambda i, j: (0, 0)),
            pl.BlockSpec((D, tn), lambda i, j: (0, j)),
        ],
        out_specs=pl.BlockSpec((tm, tn), lambda i, j: (i, j)),
        out_shape=jax.ShapeDtypeStruct((T, N), out_dtype),
        scratch_shapes=[pltpu.VMEM((tm, D), BF16)],
        compiler_params=pltpu.CompilerParams(
            dimension_semantics=("parallel", "arbitrary"), vmem_limit_bytes=VMEM_LIMIT),
        name="norm_proj",
    )(x2d, gain, w)


def _norm_proj_side_kernel(x_ref, g_ref, w_ref, ws_ref, o_ref, os_ref, xn_ref):
    @pl.when(pl.program_id(1) == 0)
    def _():
        x = x_ref[...]
        ms = jnp.mean(x * x, axis=-1, keepdims=True)
        xn_ref[...] = (x * lax.rsqrt(ms + EPS) * g_ref[...]).astype(BF16)
        os_ref[...] = jnp.dot(xn_ref[...], ws_ref[...], preferred_element_type=F32).astype(os_ref.dtype)

    o_ref[...] = jnp.dot(xn_ref[...], w_ref[...], preferred_element_type=F32).astype(o_ref.dtype)


def _norm_proj_side(x2d, gain, w, w_side, tm, tn, out_dtype):
    T, D = x2d.shape
    N, NS = w.shape[1], w_side.shape[1]
    return pl.pallas_call(
        _norm_proj_side_kernel,
        grid=(T // tm, N // tn),
        in_specs=[
            pl.BlockSpec((tm, D), lambda i, j: (i, 0)),
            pl.BlockSpec((1, D), lambda i, j: (0, 0)),
            pl.BlockSpec((D, tn), lambda i, j: (0, j)),
            pl.BlockSpec((D, NS), lambda i, j: (0, 0)),
        ],
        out_specs=[pl.BlockSpec((tm, tn), lambda i, j: (i, j)), pl.BlockSpec((tm, NS), lambda i, j: (i, 0))],
        out_shape=[jax.ShapeDtypeStruct((T, N), out_dtype), jax.ShapeDtypeStruct((T, NS), out_dtype)],
        scratch_shapes=[pltpu.VMEM((tm, D), BF16)],
        compiler_params=pltpu.CompilerParams(
            dimension_semantics=("parallel", "arbitrary"), vmem_limit_bytes=VMEM_LIMIT),
        name="norm_proj_side",
    )(x2d, gain, w, w_side)


def _dsa_kernel(qa_ref, c_ref, qi_ref, kwq_ref, kwk_ref, gate_ref, kvn_ref, wuk_ref, kn_ref, qn_ref, wuv_ref,
                o_ref, cnT_s, k_s, ki_s, key_s, khi_s, klo_s, bias_s, lg_s, j_s, qall_s, lat_s, *, S, Q, topk):
    qb = pl.program_id(1)
    nkb = qb + 1
    RB = 128
    nrb = nkb * (Q // RB)
    NT = (((1,), (1,)), ((), ()))

    @pl.when(qb == 0)
    def _prep():
        def body(j, _):
            r0 = pl.multiple_of(j * Q, Q)
            c = c_ref[pl.ds(r0, Q), :]
            cn = c * lax.rsqrt(jnp.mean(c * c, axis=-1, keepdims=True) + EPS) * kvn_ref[...]
            kk = jnp.dot(cn.astype(BF16), wuk_ref[...], preferred_element_type=F32)
            kk = kk * lax.rsqrt(jnp.mean(kk * kk, axis=-1, keepdims=True) + EPS) * kn_ref[...]
            cnT_s[j] = cn.T.astype(BF16)
            k_s[pl.ds(r0, Q), :] = kk.astype(BF16)
            ki_s[pl.ds(r0, Q), :] = kwk_ref[pl.ds(r0, Q), 0:D_I].astype(BF16)
            return 0
        lax.fori_loop(0, S // Q, body, 0)

    tpos = qb * Q + lax.broadcasted_iota(I32, (1, Q), 1)
    w_t = kwq_ref[...].T[D_I:D_I + H_I, :] * (H_I ** -0.5 * D_I ** -0.5)

    qi_heads = [qi_ref[:, h * D_I:(h + 1) * D_I].astype(BF16) for h in range(H_I)]

    def score_blk(j, _):
        r0 = pl.multiple_of(j * Q, Q)
        kib = ki_s[pl.ds(r0, Q), :]
        acc = jnp.zeros((Q, Q), F32)
        for h in range(H_I):
            s = lax.dot_general(kib, qi_heads[h], NT, preferred_element_type=F32)
            acc = acc + jnp.maximum(s, 0.0) * w_t[h:h + 1, :]
        acc = acc + 0.0
        bits = pltpu.bitcast(acc, I32)
        key = bits ^ ((bits >> 31) & np.int32(0x7FFFFFFF))
        kpos = r0 + lax.broadcasted_iota(I32, (Q, Q), 0)
        key = jnp.where(kpos <= tpos, key, INT_MIN)
        key_s[pl.ds(r0, Q), :] = key
        khi_s[pl.ds(r0, Q), :] = (key >> 16).astype(I16)
        return 0
    lax.fori_loop(0, nkb, score_blk, 0)

    def count(pred):
        def body(i, acc):
            r0 = pl.multiple_of(i * RB, RB)
            kpos = r0 + lax.broadcasted_iota(I32, (RB, Q), 0)
            return acc + jnp.where(pred(key_s[pl.ds(r0, RB), :], kpos), 1.0, 0.0)
        acc = lax.fori_loop(0, nrb, body, jnp.zeros((RB, Q), F32))
        return jnp.sum(acc, axis=0, keepdims=True)

    def count16(ref, pred):
        def body(j, acc):
            r0 = pl.multiple_of(j * Q, Q)
            return acc + jnp.where(pred(ref[pl.ds(r0, Q), :]), np.int16(1), np.int16(0))
        acc = lax.fori_loop(0, nkb, body, jnp.zeros((Q, Q), I16))
        acc = jnp.sum(acc.reshape(Q // 16, 16, Q), axis=0)
        return jnp.sum(acc.astype(F32), axis=0, keepdims=True)

    def search16(ref, base):
        def bit_body(i, prefix_u):
            cand_u = prefix_u | jnp.left_shift(np.int32(1), 15 - i)
            cand_s = (cand_u - HALF16).astype(I16)
            cnt = base + count16(ref, lambda k: k >= cand_s)
            return jnp.where(cnt >= float(topk), cand_u, prefix_u)
        return lax.fori_loop(0, 16, bit_body, jnp.zeros((1, Q), I32))

    HALF16 = np.int32(1 << 15)
    th = search16(khi_s, 0.0) - HALF16
    th16 = th.astype(I16)
    cnt_hi_gt = count16(khi_s, lambda k: k > th16)

    def lo_blk(j, _):
        r0 = pl.multiple_of(j * Q, Q)
        k = key_s[pl.ds(r0, Q), :]
        lo = (k & np.int32(0xFFFF)) - HALF16
        klo_s[pl.ds(r0, Q), :] = jnp.where((k >> 16) == th, lo, -HALF16).astype(I16)
        return 0
    lax.fori_loop(0, nkb, lo_blk, 0)
    tl_u = search16(klo_s, cnt_hi_gt)
    thr = (th << 16) | tl_u

    tl16 = (tl_u - HALF16).astype(I16)
    cnt_ge = cnt_hi_gt + count16(klo_s, lambda k: k >= tl16)
    ambiguous = (cnt_ge != float(topk)) & (tpos >= topk - 1)
    j_s[...] = jnp.full((1, Q), S, I32)

    @pl.when(jnp.max(jnp.where(ambiguous, 1.0, 0.0)) > 0.0)
    def _ties():
        need = float(topk) - count(lambda k, _: k > thr)
        nbits = int(S).bit_length() - 1

        def jbit(i, prefix):
            cand = prefix | jnp.left_shift(np.int32(1), nbits - 1 - i)
            cnt = count(lambda k, kpos: (k == thr) & (kpos < cand))
            return jnp.where(cnt < need, cand, prefix)
        jstar = lax.fori_loop(0, nbits, jbit, jnp.zeros((1, Q), I32))
        j_s[...] = jnp.where(ambiguous, jstar, S)

    jsel = j_s[...]

    def bias_blk(i, _):
        r0 = pl.multiple_of(i * RB, RB)
        k = key_s[pl.ds(r0, RB), :]
        kpos = r0 + lax.broadcasted_iota(I32, (RB, Q), 0)
        sel = (kpos <= tpos) & ((k > thr) | ((k == thr) & (kpos <= jsel)))
        bias_s[pl.ds(r0, RB), :] = jnp.where(sel, 0.0, -jnp.inf)
        return 0
    lax.fori_loop(0, nrb, bias_blk, 0)

    for h in range(H_A):
        qh = qa_ref[:, h * DH_A:(h + 1) * DH_A]
        qh = qh * lax.rsqrt(jnp.mean(qh * qh, axis=-1, keepdims=True) + EPS) * qn_ref[...] * (DH_A ** -0.5)
        qall_s[h * Q:(h + 1) * Q, :] = qh.astype(BF16)

    def pass_a(j, m):
        r0 = pl.multiple_of(j * Q, Q)
        s = lax.dot_general(k_s[pl.ds(r0, Q), :], qall_s[...], NT, preferred_element_type=F32)
        bias = bias_s[pl.ds(r0, Q), :]
        ms = []
        for h in range(H_A):
            sh = s[:, h * Q:(h + 1) * Q] + bias
            lg_s[pl.ds(r0, Q), h * Q:(h + 1) * Q] = sh
            ms.append(jnp.max(sh, axis=0, keepdims=True))
        return jnp.maximum(m, jnp.concatenate(ms, axis=1))
    m = lax.fori_loop(0, nkb, pass_a, jnp.full((1, H_A * Q), -jnp.inf, F32))

    p0 = jnp.exp(lg_s[0:Q, :] - m)
    lat_s[...] = jnp.dot(cnT_s[0], p0.astype(BF16), preferred_element_type=F32)

    def pass_b(j, l):
        r0 = pl.multiple_of(j * Q, Q)
        p = jnp.exp(lg_s[pl.ds(r0, Q), :] - m)
        lat_s[...] += jnp.dot(cnT_s[j], p.astype(BF16), preferred_element_type=F32)
        return l + jnp.sum(p, axis=0, keepdims=True)
    l = lax.fori_loop(1, nkb, pass_b, jnp.sum(p0, axis=0, keepdims=True))

    lat = (lat_s[...] * (1.0 / l)).astype(BF16)
    for h in range(H_A):
        cols = slice(h * DH_A, (h + 1) * DH_A)
        out_h = lax.dot_general(lat[:, h * Q:(h + 1) * Q], wuv_ref[h], (((0,), (0,)), ((), ())),
                                preferred_element_type=F32)
        o_ref[:, cols] = (out_h * _silu(gate_ref[:, cols])).astype(o_ref.dtype)


def _dsa(proj, kv_norm, w_uk, k_norm, q_norm, w_uv, Q=256):
    B, S, _ = proj.shape
    topk = min(INDEX_TOPK, S // 4)
    assert S % Q == 0 and topk <= Q and (S & (S - 1)) == 0
    kern = functools.partial(_dsa_kernel, S=S, Q=Q, topk=topk)
    full = lambda *shape: pl.BlockSpec(shape, lambda b, q: (0,) * len(shape))
    return pl.pallas_call(
        kern,
        grid=(B, S // Q),
        in_specs=[
            pl.BlockSpec((None, Q, MIX_A), lambda b, q: (b, q, AB_QA // MIX_A)),
            pl.BlockSpec((None, S, D_C), lambda b, q: (b, 0, AB_C // D_C)),
            pl.BlockSpec((None, Q, H_I * D_I), lambda b, q: (b, q, AB_QI // (H_I * D_I))),
            pl.BlockSpec((None, Q, LANES), lambda b, q: (b, q, AB_KW // LANES)),
            pl.BlockSpec((None, S, LANES), lambda b, q: (b, 0, AB_KW // LANES)),
            pl.BlockSpec((None, Q, MIX_A), lambda b, q: (b, q, AB_GATE // MIX_A)),
            full(1, D_C), full(D_C, DH_A), full(1, DH_A), full(1, DH_A), full(H_A, D_C, DH_A),
        ],
        out_specs=pl.BlockSpec((None, Q, MIX_A), lambda b, q: (b, q, 0)),
        out_shape=jax.ShapeDtypeStruct((B, S, MIX_A), BF16),
        scratch_shapes=[
            pltpu.VMEM((S // Q, D_C, Q), BF16),
            pltpu.VMEM((S, DH_A), BF16),
            pltpu.VMEM((S, D_I), BF16),
            pltpu.VMEM((S, Q), I32),
            pltpu.VMEM((S, Q), I16),
            pltpu.VMEM((S, Q), I16),
            pltpu.VMEM((S, Q), F32),
            pltpu.VMEM((S, H_A * Q), F32),
            pltpu.VMEM((1, Q), I32),
            pltpu.VMEM((H_A * Q, DH_A), BF16),
            pltpu.VMEM((D_C, H_A * Q), F32),
        ],
        compiler_params=pltpu.CompilerParams(
            dimension_semantics=("parallel", "arbitrary"), vmem_limit_bytes=VMEM_LIMIT),
        name="dsa_attention",
    )(proj, proj, proj, proj, proj, proj, kv_norm, w_uk, k_norm, q_norm, w_uv)


def _ret_kernel(lg_ref, q_ref, k_ref, v_ref, gate_ref, cos_ref, sin_ref, g_ref, o_ref, *, S, C):
    h = pl.program_id(1)
    lg = lg_ref[h]
    NT = (((1,), (1,)), ((), ()))
    TN = (((0,), (0,)), ((), ()))
    N = S // C
    ii = lax.broadcasted_iota(I32, (C, C), 0)
    jj = lax.broadcasted_iota(I32, (C, C), 1)
    diff = (ii - jj).astype(F32)
    dmask = jnp.where(diff >= 0, jnp.exp(jnp.maximum(diff, 0.0) * lg), 0.0)
    icol = lax.broadcasted_iota(I32, (C, 1), 0).astype(F32)
    k_dec = jnp.exp((C - 1 - icol) * lg)
    q_dec = jnp.exp((icol + 1.0) * lg)
    chunk_decay = jnp.exp(jnp.full((1, DV_B), float(C), F32) * lg)

    def rot(x, cos, sin):
        return x * cos + pltpu.roll(x, DK_B // 2, axis=1) * sin

    q, k, v = [], [], []
    for n in range(N):
        rows = slice(n * C, (n + 1) * C)
        cos, sin = cos_ref[rows, :], sin_ref[rows, :]
        q.append(rot(q_ref[rows, :], cos, sin))
        k.append(rot(k_ref[rows, :], cos, sin) * (DK_B ** -0.5))
        v.append(v_ref[rows, :])
    inner = [lax.dot_general(q[n], k[n], NT, preferred_element_type=F32) * dmask for n in range(N)]
    kv = [lax.dot_general(k[n] * k_dec, v[n], TN, preferred_element_type=F32) for n in range(N - 1)]
    out = [jnp.dot(inner[n], v[n], preferred_element_type=F32) for n in range(N)]
    st = kv[0]
    for n in range(1, N):
        out[n] = out[n] + jnp.dot(q[n] * q_dec, st, preferred_element_type=F32)
        if n + 1 < N:
            st = st * chunk_decay + kv[n]
    for n in range(N):
        rows = slice(n * C, (n + 1) * C)
        o = out[n] * lax.rsqrt(jnp.mean(out[n] * out[n], axis=-1, keepdims=True) + EPS) * g_ref[...]
        o_ref[rows, :] = (o * _silu(gate_ref[rows, :])).astype(o_ref.dtype)


def _retention(proj, ret_norm):
    B, S, _ = proj.shape
    C = RET_CHUNK
    pos = jnp.arange(S, dtype=F32)
    inv_freq = 1.0 / (RET_THETA ** jnp.linspace(0.0, 1.0, DK_B // 2, dtype=F32))
    ang = pos[:, None] * inv_freq[None, :]
    cos2 = jnp.concatenate([jnp.cos(ang), jnp.cos(ang)], axis=-1)
    sin2 = jnp.concatenate([-jnp.sin(ang), jnp.sin(ang)], axis=-1)
    log_gamma = jnp.log1p(-jnp.exp2(-5.0 - jnp.arange(H_B, dtype=F32)))
    kern = functools.partial(_ret_kernel, S=S, C=C)
    return pl.pallas_call(
        kern,
        grid_spec=pltpu.PrefetchScalarGridSpec(
            num_scalar_prefetch=1,
            grid=(B, H_B),
            in_specs=[
                pl.BlockSpec((None, S, DK_B), lambda b, h, lg: (b, 0, AB_QB // DK_B + h)),
                pl.BlockSpec((None, S, DK_B), lambda b, h, lg: (b, 0, AB_KB // DK_B + h)),
                pl.BlockSpec((None, S, DV_B), lambda b, h, lg: (b, 0, AB_VB // DV_B + h)),
                pl.BlockSpec((None, S, DV_B), lambda b, h, lg: (b, 0, (AB_GATE + MIX_A) // DV_B + h)),
                pl.BlockSpec((S, DK_B), lambda b, h, lg: (0, 0)),
                pl.BlockSpec((S, DK_B), lambda b, h, lg: (0, 0)),
                pl.BlockSpec((None, 1, DV_B), lambda b, h, lg: (h, 0, 0)),
            ],
            out_specs=pl.BlockSpec((None, S, DV_B), lambda b, h, lg: (b, 0, h)),
        ),
        out_shape=jax.ShapeDtypeStruct((B, S, MIX_B), BF16),
        compiler_params=pltpu.CompilerParams(
            dimension_semantics=("parallel", "arbitrary"), vmem_limit_bytes=VMEM_LIMIT),
        name="retention",
    )(log_gamma, proj, proj, proj, proj, cos2, sin2, ret_norm.reshape(H_B, 1, DV_B))


def _ab_out_kernel(a_ref, b_ref, x_ref, w_ref, o_ref):
    o_ref[...] = (x_ref[...] + jnp.dot(a_ref[...], w_ref[:MIX_A, :], preferred_element_type=F32)
                  + jnp.dot(b_ref[...], w_ref[MIX_A:, :], preferred_element_type=F32))


def _ab_out(a2d, b2d, x2d, w, tm=1024, tn=1024):
    T, D = x2d.shape
    return pl.pallas_call(
        _ab_out_kernel,
        grid=(T // tm, D // tn),
        in_specs=[
            pl.BlockSpec((tm, MIX_A), lambda i, j: (i, 0)),
            pl.BlockSpec((tm, MIX_B), lambda i, j: (i, 0)),
            pl.BlockSpec((tm, tn), lambda i, j: (i, j)),
            pl.BlockSpec((MIX_AB, tn), lambda i, j: (0, j)),
        ],
        out_specs=pl.BlockSpec((tm, tn), lambda i, j: (i, j)),
        out_shape=jax.ShapeDtypeStruct((T, D), F32),
        compiler_params=pltpu.CompilerParams(
            dimension_semantics=("parallel", "arbitrary"), vmem_limit_bytes=VMEM_LIMIT),
        name="ab_out_proj",
    )(a2d, b2d, x2d, w)


def _gdn_gates_kernel(ba_ref, alog_ref, dt_ref, beta_ref, gc_ref, *, S, C):
    ba = ba_ref[...].astype(F32)
    beta_ref[...] = _sigmoid(ba)
    xg = ba + dt_ref[...]
    softplus = jnp.maximum(xg, 0.0) + jnp.log(1.0 + jnp.exp(-jnp.abs(xg)))
    g_all = -jnp.exp(alog_ref[...]) * softplus
    rc = lax.broadcasted_iota(I32, (S, LANES), 0) & (C - 1)
    sh = 1
    while sh < C:
        g_all = g_all + jnp.where(rc >= sh, pltpu.roll(g_all, sh, axis=0), 0.0)
        sh *= 2
    gc_ref[...] = g_all


def _gdn_gates(ba, a_log, dt_bias):
    B, S, _ = ba.shape
    alog_row = jnp.zeros((1, LANES), F32).at[0, H_V:2 * H_V].set(a_log)
    dt_row = jnp.zeros((1, LANES), F32).at[0, H_V:2 * H_V].set(dt_bias)
    row = pl.BlockSpec((1, LANES), lambda b: (0, 0))
    out = pl.BlockSpec((None, S, LANES), lambda b: (b, 0, 0))
    return pl.pallas_call(
        functools.partial(_gdn_gates_kernel, S=S, C=DN_CHUNK),
        grid=(B,),
        in_specs=[pl.BlockSpec((None, S, LANES), lambda b: (b, 0, 0)), row, row],
        out_specs=[out, out],
        out_shape=[jax.ShapeDtypeStruct((B, S, LANES), F32)] * 2,
        compiler_params=pltpu.CompilerParams(dimension_semantics=("parallel",), vmem_limit_bytes=VMEM_LIMIT),
        name="gdn_gates",
    )(ba, alog_row, dt_row)


def _gdn_kernel(q_ref, k_ref, v_ref, z_ref, beta_ref, gc_ref, cwq_ref, cwk_ref, cwv_ref, on_ref,
                o_ref, qn_s, kn_s, va_s, bb_s, gb_s, lhs_s, au_s, ku_s, gl_s, *, S, C, GC, U):
    g = pl.program_id(1)
    NG = S // (C * GC)
    REP = H_V // H_QK
    VW = REP * DV_C
    NT = (((1,), (1,)), ((), ()))
    TN = (((0,), (0,)), ((), ()))

    GR = GC * C
    HALO = 16

    def prep(gi):
        u, r0 = gi // NG, (gi % NG) * GR

        def conv_silu(ref, w, width):
            cols = slice(u * width, (u + 1) * width)
            w = w[:, cols]
            if r0 == 0:
                xw = jnp.concatenate([jnp.zeros((HALO, width), F32), ref[0:GR, cols].astype(F32)], axis=0)
            else:
                xw = ref[r0 - HALO:r0 + GR, cols].astype(F32)
            y = xw[HALO:] * w[CONV_K - 1:CONV_K, :]
            for d in range(1, CONV_K):
                y = y + pltpu.roll(xw, d, axis=0)[HALO:] * w[CONV_K - 1 - d:CONV_K - d, :]
            return _silu(y)

        def l2n(x):
            return x * lax.rsqrt(jnp.sum(x * x, axis=-1, keepdims=True) + EPS)

        qn_s[u, r0:r0 + GR, :] = l2n(conv_silu(q_ref, cwq_ref[...], DK_C)) * (DK_C ** -0.5)
        kn_s[u, r0:r0 + GR, :] = l2n(conv_silu(k_ref, cwk_ref[...], DK_C))
        va_s[u, r0:r0 + GR, :] = conv_silu(v_ref, cwv_ref[...], VW)
        lane = lax.broadcasted_iota(I32, (GR, LANES), 1)
        for j in range(REP):
            hv = (g * U + u) * REP + j
            bcol = jnp.sum(jnp.where(lane == hv, beta_ref[r0:r0 + GR, :], 0.0), axis=-1, keepdims=True)
            gcol = jnp.sum(jnp.where(lane == hv + H_V, gc_ref[r0:r0 + GR, :], 0.0), axis=-1, keepdims=True)
            bb_s[u, j, r0:r0 + GR, :] = jnp.broadcast_to(bcol, (GR, LANES))
            gb_s[u, j, r0:r0 + GR, :] = jnp.broadcast_to(gcol, (GR, LANES))

    assert REP * C == LANES
    ii = lax.broadcasted_iota(I32, (REP * C, REP * C), 0)
    jj = lax.broadcasted_iota(I32, (REP * C, REP * C), 1)
    same_head = (ii >= C) == (jj >= C)
    tri = same_head & (ii >= jj)
    strict = same_head & (ii > jj)
    eye = jnp.where(ii == jj, 1.0, 0.0)

    dot = functools.partial(jnp.dot, preferred_element_type=F32)
    chains = [(c, j) for c in range(GC) for j in range(REP)]

    def row0(gi, c):
        return ((gi % NG) * GC + c) * C

    def scan_step(gi, c, states):
        u, r0 = gi // NG, row0(gi, c)
        for j in range(REP):
            i = c * REP + j
            cols = slice(u * VW + j * DV_C, u * VW + (j + 1) * DV_C)
            r = dot(lhs_s[i], states[j].astype(BF16))
            o = r[DK_C:] + au_s[i]
            states[j] = states[j] * gl_s[i, 0:1, :] + ku_s[i] - r[:DK_C]
            o = o * lax.rsqrt(jnp.mean(o * o, axis=-1, keepdims=True) + EPS) * on_ref[...]
            o = o * _silu(z_ref[r0:r0 + C, cols].astype(F32))
            o_ref[r0:r0 + C, cols] = o.astype(o_ref.dtype)

    def stage(gi, pending):
        pending = list(pending)

        def tick():
            if pending:
                pending.pop(0)()
        u = gi // NG
        r0s = [row0(gi, c) for c in range(GC)]
        qc = [qn_s[u, r0s[c]:r0s[c] + C, :] for c in range(GC)]
        kc = [kn_s[u, r0s[c]:r0s[c] + C, :] for c in range(GC)]
        qkk = [lax.dot_general(jnp.concatenate([qc[c], kc[c]], axis=0).astype(BF16),
                               jnp.concatenate([kc[c], kc[c]], axis=0).astype(BF16),
                               NT, preferred_element_type=F32) for c in range(GC)]
        tick()
        gcb, bb, p, attn = {}, {}, [], []
        for c in range(GC):
            for j in range(REP):
                gcb[c, j] = gb_s[u, j, r0s[c]:r0s[c] + C, :]
                bb[c, j] = bb_s[u, j, r0s[c]:r0s[c] + C, :]
            g2 = jnp.concatenate([gcb[c, j] for j in range(REP)], axis=0)
            b2 = jnp.concatenate([bb[c, j] for j in range(REP)], axis=0)
            gcr = jnp.sum(jnp.where(ii == jj, g2, 0.0), axis=0, keepdims=True)
            decay = jnp.exp(jnp.where(tri, g2 - gcr, -jnp.inf))
            kk2 = jnp.concatenate([qkk[c][C:]] * REP, axis=0)
            qk2 = jnp.concatenate([qkk[c][:C]] * REP, axis=0)
            p.append(jnp.where(strict, -(kk2 * b2 * decay), 0.0))
            attn.append(qk2 * decay)
        qm = [dot(p[c].astype(BF16), p[c].astype(BF16)) for c in range(GC)]
        tick()
        sm = [eye + p[c] for c in range(GC)]
        for _ in range(4):
            r = [dot(qm[c].astype(BF16), jnp.concatenate([qm[c], sm[c]], axis=1).astype(BF16)) for c in range(GC)]
            tick()
            qm = [r[c][:, :REP * C] for c in range(GC)]
            sm = [sm[c] + r[c][:, REP * C:] for c in range(GC)]
        tinv = [sm[c] + dot(qm[c].astype(BF16), sm[c].astype(BF16)) for c in range(GC)]
        tick()
        uw2, glast = [], {}
        for c in range(GC):
            rhs = []
            for j in range(REP):
                vb = va_s[u, r0s[c]:r0s[c] + C, j * DV_C:(j + 1) * DV_C] * bb[c, j]
                kbg = kc[c] * bb[c, j] * jnp.exp(gcb[c, j])
                rhs.append(jnp.concatenate([vb, kbg], axis=1))
                glast[c, j] = gcb[c, j][C - 1:C, :]
            uw2.append(dot(tinv[c].astype(BF16), jnp.concatenate(rhs, axis=0).astype(BF16)))
        tick()
        uw = {(c, j): uw2[c][j * C:(j + 1) * C] for c, j in chains}
        ktuw = {(c, j): lax.dot_general(kc[c] * jnp.exp(glast[c, j] - gcb[c, j]), uw[c, j], TN,
                                        preferred_element_type=F32) for c, j in chains}
        tick()
        atuw2 = [dot(attn[c].astype(BF16), uw2[c].astype(BF16)) for c in range(GC)]
        atuw = {(c, j): atuw2[c][j * C:(j + 1) * C] for c, j in chains}
        while pending:
            tick()
        for c, j in chains:
            i = c * REP + j
            lhs_s[i] = jnp.concatenate([ktuw[c, j][:, DV_C:], qc[c] * jnp.exp(gcb[c, j]) - atuw[c, j][:, DV_C:]],
                                       axis=0).astype(BF16)
            au_s[i] = atuw[c, j][:, :DV_C]
            ku_s[i] = ktuw[c, j][:, :DV_C]
            gl_s[i] = jnp.broadcast_to(jnp.exp(glast[c, j]), (8, LANES))

    states = [[jnp.zeros((DK_C, DV_C), F32) for _ in range(REP)] for _ in range(U)]
    prep(0)
    for gi in range(U * NG):
        if gi + 1 < U * NG:
            prep(gi + 1)
        stage(gi, [functools.partial(scan_step, gi - 1, c, states[(gi - 1) // NG]) for c in range(GC)] if gi else [])
    for c in range(GC):
        scan_step(U * NG - 1, c, states[U - 1])


def _gdn(proj, ba, conv_w, a_log, dt_bias, out_norm):
    B, S, _ = proj.shape
    C = DN_CHUNK
    REP = H_V // H_QK
    VW = REP * DV_C
    GC = 8
    U = 2
    assert S % (C * GC) == 0 and H_QK % U == 0
    beta_all, gc_all = _gdn_gates(ba, a_log, dt_bias)
    kern = functools.partial(_gdn_kernel, S=S, C=C, GC=GC, U=U)
    KW, UW = U * DK_C, U * VW
    return pl.pallas_call(
        kern,
        grid=(B, H_QK // U),
        in_specs=[
            pl.BlockSpec((None, S, KW), lambda b, g: (b, 0, DN_Q // KW + g)),
            pl.BlockSpec((None, S, KW), lambda b, g: (b, 0, DN_K // KW + g)),
            pl.BlockSpec((None, S, UW), lambda b, g: (b, 0, DN_V // UW + g)),
            pl.BlockSpec((None, S, UW), lambda b, g: (b, 0, DN_Z // UW + g)),
            pl.BlockSpec((None, S, LANES), lambda b, g: (b, 0, 0)),
            pl.BlockSpec((None, S, LANES), lambda b, g: (b, 0, 0)),
            pl.BlockSpec((CONV_K, KW), lambda b, g: (0, DN_Q // KW + g)),
            pl.BlockSpec((CONV_K, KW), lambda b, g: (0, DN_K // KW + g)),
            pl.BlockSpec((CONV_K, UW), lambda b, g: (0, DN_V // UW + g)),
            pl.BlockSpec((1, DV_C), lambda b, g: (0, 0)),
        ],
        out_specs=pl.BlockSpec((None, S, UW), lambda b, g: (b, 0, g)),
        out_shape=jax.ShapeDtypeStruct((B, S, MIX_C), BF16),
        scratch_shapes=[
            pltpu.VMEM((U, S, DK_C), F32),
            pltpu.VMEM((U, S, DK_C), F32),
            pltpu.VMEM((U, S, VW), F32),
            pltpu.VMEM((U, REP, S, LANES), F32),
            pltpu.VMEM((U, REP, S, LANES), F32),
            pltpu.VMEM((GC * REP, DK_C + C, DK_C), BF16),
            pltpu.VMEM((GC * REP, C, DV_C), F32),
            pltpu.VMEM((GC * REP, DK_C, DV_C), F32),
            pltpu.VMEM((GC * REP, 8, LANES), F32),
        ],
        compiler_params=pltpu.CompilerParams(
            dimension_semantics=("parallel", "arbitrary"), vmem_limit_bytes=VMEM_LIMIT),
        name="gated_deltanet",
    )(proj, proj, proj, proj, beta_all, gc_all, conv_w, conv_w, conv_w, out_norm.reshape(1, DV_C))


def _dn_out_kernel(o_in_ref, x_ref, w_ref, o_ref):
    o_ref[...] = x_ref[...] + jnp.dot(o_in_ref[...], w_ref[...], preferred_element_type=F32)


def _dn_out(o2d, x2d, w, tm=1024, tn=1024):
    T, D = x2d.shape
    K = o2d.shape[1]
    return pl.pallas_call(
        _dn_out_kernel,
        grid=(T // tm, D // tn),
        in_specs=[
            pl.BlockSpec((tm, K), lambda i, j: (i, 0)),
            pl.BlockSpec((tm, tn), lambda i, j: (i, j)),
            pl.BlockSpec((K, tn), lambda i, j: (0, j)),
        ],
        out_specs=pl.BlockSpec((tm, tn), lambda i, j: (i, j)),
        out_shape=jax.ShapeDtypeStruct((T, D), F32),
        compiler_params=pltpu.CompilerParams(
            dimension_semantics=("parallel", "arbitrary"), vmem_limit_bytes=VMEM_LIMIT),
        name="dn_out_proj",
    )(o2d, x2d, w)


def _ab_weight(w):
    D = w.shape[0]
    w = w.astype(BF16)
    o = np.cumsum([0, MIX_A, D_C, H_I * D_I, D_I, H_I, H_B * DK_B, H_B * DK_B, MIX_B, MIX_AB])
    q_a, c, q_i, k_i, w_i, q_b, k_b, v_b, gate = [w[:, o[i]:o[i + 1]] for i in range(9)]
    pad_kw = jnp.zeros((D, LANES - D_I - H_I), w.dtype)
    pad_end = jnp.zeros((D, AB_NP - AB_KW - LANES), w.dtype)
    return jnp.concatenate([q_a, v_b, gate, q_i, q_b, k_b, c, k_i, w_i, pad_kw, pad_end], axis=1)


def _dn_weight(w):
    side = jnp.pad(w[:, DN_NP:], ((0, 0), (0, LANES - (w.shape[1] - DN_NP))))
    return w[:, :DN_NP].astype(BF16), side.astype(BF16)


def kernel(x, ab_norm, ab_w_in, ab_kv_norm, ab_w_uk, ab_w_uv, ab_q_norm, ab_k_norm, ab_ret_norm, ab_w_out,
           dn_norm, dn_w_in, dn_conv, dn_a_log, dn_dt_bias, dn_out_norm, dn_w_out):
    B, S, D = x.shape
    T = B * S
    depth = ab_norm.shape[0] + dn_norm.shape[0]
    h = x.astype(F32).reshape(T, D)
    for layer in range(depth):
        j = layer // 2
        if layer % 2 == 0:
            proj = _norm_proj(h, ab_norm[j].reshape(1, D), _ab_weight(ab_w_in[j]), tm=1024, tn=2048)
            proj3 = proj.reshape(B, S, AB_NP)
            a_out = _dsa(proj3, ab_kv_norm[j].reshape(1, D_C), ab_w_uk[j].astype(BF16),
                         ab_k_norm[j].reshape(1, DH_A), ab_q_norm[j].reshape(1, DH_A), ab_w_uv[j].astype(BF16))
            b_out = _retention(proj3, ab_ret_norm[j])
            h = _ab_out(a_out.reshape(T, MIX_A), b_out.reshape(T, MIX_B), h, ab_w_out[j].astype(BF16))
        else:
            w_main, w_side = _dn_weight(dn_w_in[j])
            proj, ba = _norm_proj_side(h, dn_norm[j].reshape(1, D), w_main, w_side, tm=1024, tn=2048, out_dtype=BF16)
            o = _gdn(proj.reshape(B, S, DN_NP), ba.reshape(B, S, LANES), dn_conv[j], dn_a_log[j], dn_dt_bias[j],
                     dn_out_norm[j])
            h = _dn_out(o.reshape(T, MIX_C), h, dn_w_out[j].astype(BF16))
    return h.reshape(B, S, D).astype(x.dtype)
```
